```python
import jax, jax.numpy as jnp
from jax import lax
import numpy as np

D_MODEL = 1024
BATCH = 2
SEQ = 8192
DEPTH = 4
DEC_BATCH = 32
DEC_SEQ = 1
PAST_LEN = 8192
PAGE_SIZE = 128

POOL_WINDOWS = (2, 4, 8, 16)
N_POOL_GROUPS = len(POOL_WINDOWS)
POOL_GC = D_MODEL // N_POOL_GROUPS
POOL_STATE = max(POOL_WINDOWS) - 1
ATTN_PATTERNS = ((128, 1), (512, 4), (2048, 16))
N_ATTN_GROUPS = len(ATTN_PATTERNS)
HEAD_DIM = 64
HEADS_PER_GROUP = 8
ATTN_OUT = HEADS_PER_GROUP * HEAD_DIM
QKV_WIDTH = 3 * N_ATTN_GROUPS * ATTN_OUT
Q_BLOCK = 128
ROPE_THETA = 10000.0
D_FF = 4 * D_MODEL
EPS = 1e-6
NEG = -1e30
N_POOL_LAYERS = (DEPTH + 1) // 2
N_ATTN_LAYERS = DEPTH // 2

kernel_name = "hybrid_pool_dilated_attn_decoder_step"


def rmsnorm(x, g):
    xf = x.astype(jnp.float32)
    y = xf * lax.rsqrt(jnp.mean(xf * xf, axis=-1, keepdims=True) + EPS)
    return (y * g.astype(jnp.float32)).astype(x.dtype)


def rope(x, pos):
    half = HEAD_DIM // 2
    inv = ROPE_THETA ** (-jnp.arange(half, dtype=jnp.float32) * 2.0 / HEAD_DIM)
    ang = pos.astype(jnp.float32)[:, None] * inv[None, :]
    cos = jnp.cos(ang)[None, :, None, None, :]
    sin = jnp.sin(ang)[None, :, None, None, :]
    xf = x.astype(jnp.float32)
    x1, x2 = xf[..., :half], xf[..., half:]
    return jnp.concatenate([x1 * cos - x2 * sin, x2 * cos + x1 * sin], axis=-1).astype(x.dtype)


def pool_mix(xn, prev, p0, w_pool, scale):
    B, S, _ = xn.shape
    ext = jnp.concatenate([prev.astype(xn.dtype), xn], axis=1)
    cs = jnp.cumsum(ext.astype(jnp.float32), axis=1)
    cs = jnp.concatenate([jnp.zeros((B, 1, D_MODEL), jnp.float32), cs], axis=1)
    pos = p0 + jnp.arange(S)
    P = POOL_STATE
    xf = xn.astype(jnp.float32)
    diffs = []
    for g, w in enumerate(POOL_WINDOWS):
        sl = slice(g * POOL_GC, (g + 1) * POOL_GC)
        hi = cs[:, P + 1:P + 1 + S, sl]
        lo = cs[:, P + 1 - w:P + 1 - w + S, sl]
        cnt = jnp.minimum(pos + 1, w).astype(jnp.float32)[None, :, None]
        diffs.append((hi - lo) / cnt - xf[..., sl])
    d = jnp.stack(diffs, axis=2)
    y = jnp.einsum('bsgc,gce->bsge', d, w_pool.astype(jnp.float32)).reshape(B, S, D_MODEL)
    y = (y * scale.astype(jnp.float32)).astype(xn.dtype)
    return y, ext[:, ext.shape[1] - P:]


def dilated_group(q, ext, p0, n_buf, window, dilation):
    B, S, H, Dh = q.shape
    n_keys = window // dilation + 1
    Q = Q_BLOCK if S % Q_BLOCK == 0 else S
    nb = S // Q
    dist = jnp.arange(n_keys) * dilation
    qi = jnp.arange(Q)
    scale = HEAD_DIM ** -0.5
    band_len = n_buf + Q

    def block(t0):
        qb = lax.dynamic_slice_in_dim(q, t0, Q, axis=1).astype(jnp.float32)
        band = lax.dynamic_slice_in_dim(ext, t0, band_len, axis=1)
        idx = n_buf + qi[:, None] - dist[None, :]
        valid = ((p0 + t0 + qi[:, None] - dist[None, :]) >= 0) & (idx >= 0)
        kv = jnp.take(band, jnp.clip(idx, 0, band_len - 1).reshape(-1), axis=1)
        kv = kv.reshape(B, Q, n_keys, 2, H, Dh).astype(jnp.float32)
        s = jnp.einsum('bqhd,bqkhd->bqhk', qb, kv[:, :, :, 0]) * scale
        s = jnp.where(valid[None, :, None, :], s, NEG)
        m = jnp.max(s, axis=-1)
        p = jnp.exp(s - m[..., None])
        l = jnp.sum(p, axis=-1)
        o = jnp.einsum('bqhk,bqkhd->bqhd', p, kv[:, :, :, 1])
        return o, m, l

    o, m, l = lax.map(block, jnp.arange(nb) * Q)
    o = jnp.transpose(o, (1, 0, 2, 3, 4)).reshape(B, S, H, Dh)
    m = jnp.transpose(m, (1, 0, 2, 3)).reshape(B, S, H)
    l = jnp.transpose(l, (1, 0, 2, 3)).reshape(B, S, H)
    return o, m, l


def attn_mix(xn, bufs, out_lens, p0, w_qkv, q_norm, k_norm, w_o):
    B, S, _ = xn.shape
    pos = p0 + jnp.arange(S)
    qkv = jnp.einsum('bsd,de->bse', xn, w_qkv).reshape(B, S, 3, N_ATTN_GROUPS, HEADS_PER_GROUP, HEAD_DIM)
    q = rope(rmsnorm(qkv[:, :, 0], q_norm[:, None, :]), pos)
    k = rope(rmsnorm(qkv[:, :, 1], k_norm[:, None, :]), pos)
    v = qkv[:, :, 2]
    kv_new = jnp.stack([k, v], axis=2)
    outs, ms, ls, new_bufs = [], [], [], []
    for g, (window, dilation) in enumerate(ATTN_PATTERNS):
        buf = bufs[g]
        ext = jnp.concatenate([buf, kv_new[:, :, :, g].astype(buf.dtype)], axis=1)
        o, m, l = dilated_group(q[:, :, g], ext, p0, buf.shape[1], window, dilation)
        outs.append(o); ms.append(m); ls.append(l)
        new_bufs.append(ext[:, ext.shape[1] - out_lens[g]:])
    m_all = jnp.stack(ms)
    wgt = jnp.exp(m_all - jnp.max(m_all, axis=0, keepdims=True))
    den = jnp.sum(wgt * jnp.stack(ls), axis=0)
    num = jnp.sum(wgt[..., None] * jnp.stack(outs), axis=0)
    out = (num / den[..., None]).reshape(B, S, ATTN_OUT).astype(xn.dtype)
    return jnp.einsum('bse,ed->bsd', out, w_o), new_bufs


def trunk(x, pool_prev, kv_prev, out_lens, p0, norm_mix, norm_mlp, pool_w, pool_scale,
          attn_w_qkv, attn_q_norm, attn_k_norm, attn_w_o, mlp_w_up, mlp_w_down):
    pool_new, kv_new = [], []
    for layer in range(DEPTH):
        li = layer // 2
        h = rmsnorm(x, norm_mix[layer])
        if layer % 2 == 0:
            y, st = pool_mix(h, pool_prev[li], p0, pool_w[li], pool_scale[li])
            pool_new.append(st)
        else:
            y, bufs = attn_mix(h, [c[li] for c in kv_prev], out_lens, p0,
                               attn_w_qkv[li], attn_q_norm[li], attn_k_norm[li], attn_w_o[li])
            kv_new.append(bufs)
        x = x + y
        h = rmsnorm(x, norm_mlp[layer])
        x = x + jnp.einsum('bsf,fd->bsd', jnp.square(jax.nn.relu(jnp.einsum('bsd,df->bsf', h, mlp_w_up[layer]))),
                           mlp_w_down[layer])
    pool_st = jnp.stack(pool_new)
    kv_st = [jnp.stack([kv_new[l][g] for l in range(len(kv_new))]) for g in range(N_ATTN_GROUPS)]
    return x, pool_st, kv_st


def setup_inputs(seed: int = 0) -> dict:
    key = jax.random.key(seed)
    ks = jax.random.split(key, 16)
    f32 = jnp.float32
    def nrm(k, shape, s):
        return jax.random.normal(k, shape, f32) * s
    n_buf = [min(w, PAST_LEN) for w, _ in ATTN_PATTERNS]
    return {
        "x_prompt": nrm(ks[0], (BATCH, SEQ, D_MODEL), 1.0),
        "x_sample": nrm(ks[1], (DEC_BATCH, DEC_SEQ, D_MODEL), 1.0),
        "state_pool": nrm(ks[2], (N_POOL_LAYERS, DEC_BATCH, POOL_STATE, D_MODEL), 1.0),
        "cache_kv_w128": nrm(ks[3], (N_ATTN_LAYERS, DEC_BATCH, n_buf[0], 2, HEADS_PER_GROUP, HEAD_DIM), 1.0),
        "cache_kv_w512": nrm(ks[4], (N_ATTN_LAYERS, DEC_BATCH, n_buf[1], 2, HEADS_PER_GROUP, HEAD_DIM), 1.0),
        "cache_kv_w2048": nrm(ks[5], (N_ATTN_LAYERS, DEC_BATCH, n_buf[2], 2, HEADS_PER_GROUP, HEAD_DIM), 1.0),
        "norm_mix": 1.0 + nrm(ks[6], (DEPTH, D_MODEL), 0.05),
        "norm_mlp": 1.0 + nrm(ks[7], (DEPTH, D_MODEL), 0.05),
        "pool_w": nrm(ks[8], (N_POOL_LAYERS, N_POOL_GROUPS, POOL_GC, POOL_GC), POOL_GC ** -0.5),
        "pool_scale": 1.0 + nrm(ks[9], (N_POOL_LAYERS, D_MODEL), 0.1),
        "attn_w_qkv": nrm(ks[10], (N_ATTN_LAYERS, D_MODEL, QKV_WIDTH), D_MODEL ** -0.5),
        "attn_q_norm": 1.0 + nrm(ks[11], (N_ATTN_LAYERS, N_ATTN_GROUPS, HEAD_DIM), 0.05),
        "attn_k_norm": 1.0 + nrm(ks[12], (N_ATTN_LAYERS, N_ATTN_GROUPS, HEAD_DIM), 0.05),
        "attn_w_o": nrm(ks[13], (N_ATTN_LAYERS, ATTN_OUT, D_MODEL), ATTN_OUT ** -0.5),
        "mlp_w_up": nrm(ks[14], (DEPTH, D_MODEL, D_FF), D_MODEL ** -0.5),
        "mlp_w_down": nrm(ks[15], (DEPTH, D_FF, D_MODEL), 0.5 * D_FF ** -0.5),
    }


def reference(x_prompt, x_sample, state_pool, cache_kv_w128, cache_kv_w512, cache_kv_w2048,
              norm_mix, norm_mlp, pool_w, pool_scale, attn_w_qkv, attn_q_norm, attn_k_norm,
              attn_w_o, mlp_w_up, mlp_w_down):
    weights = (norm_mix, norm_mlp, pool_w, pool_scale, attn_w_qkv, attn_q_norm, attn_k_norm,
               attn_w_o, mlp_w_up, mlp_w_down)
    B, S, _ = x_prompt.shape
    pool_prev_p = jnp.zeros((N_POOL_LAYERS, B, POOL_STATE, D_MODEL), x_prompt.dtype)
    kv_prev_p = [jnp.zeros((N_ATTN_LAYERS, B, w, 2, HEADS_PER_GROUP, HEAD_DIM), x_prompt.dtype)
                 for w, _ in ATTN_PATTERNS]
    out_lens_p = tuple(min(w, S) for w, _ in ATTN_PATTERNS)
    y_prompt, pool_p, kv_p = trunk(x_prompt, pool_prev_p, kv_prev_p, out_lens_p, 0, *weights)
    kv_prev_s = [cache_kv_w128, cache_kv_w512, cache_kv_w2048]
    out_lens_s = tuple(c.shape[2] for c in kv_prev_s)
    y_sample, pool_s, kv_s = trunk(x_sample, state_pool, kv_prev_s, out_lens_s, PAST_LEN, *weights)
    return (y_prompt, y_sample, pool_p, pool_s, kv_p[0], kv_s[0], kv_p[1], kv_s[1], kv_p[2], kv_s[2])
```

```python
import functools

import jax
import jax.numpy as jnp
from jax import lax
from jax.experimental import pallas as pl
from jax.experimental.pallas import tpu as pltpu

F32 = jnp.float32
BF16 = jnp.bfloat16

D_MODEL = 1024
D_FF = 4 * D_MODEL
DEPTH = 4
POOL_WINDOWS = (2, 4, 8, 16)
POOL_GC = D_MODEL // len(POOL_WINDOWS)
POOL_STATE = max(POOL_WINDOWS) - 1
ATTN_PATTERNS = ((128, 1), (512, 4), (2048, 16))
N_GROUPS = len(ATTN_PATTERNS)
HEAD_DIM = 64
HEADS = 8
GROUP_W = HEADS * HEAD_DIM
QKV_W = 3 * N_GROUPS * GROUP_W
N_KEYS_M1 = 128
ROPE_THETA = 10000.0
PAST_LEN = 8192
EPS = 1e-6
NEG = -1e30

LANES = 128
MXU_DIM = 256
VMEM_LIMIT = 56 * 1024 * 1024
ROW_TILE = 512
FF_CHUNK = 1024
ATTN_TQ = 512
Q_SUB = 128
TAIL = max(w for w, _ in ATTN_PATTERNS)


def _params(*sem):
    return pltpu.CompilerParams(dimension_semantics=sem, vmem_limit_bytes=VMEM_LIMIT)


def _rmsnorm(x, g):
    return x * lax.rsqrt(jnp.mean(x * x, axis=-1, keepdims=True) + EPS) * g


def _mlp_kernel(x_ref, g_ref, wu_ref, wd_ref, o_ref, a_ref):
    x = x_ref[...]
    h = _rmsnorm(x, g_ref[...]).astype(BF16)
    for c in range(D_FF // FF_CHUNK):
        cols = slice(c * FF_CHUNK, (c + 1) * FF_CHUNK)
        u = jnp.dot(h, wu_ref[:, cols], preferred_element_type=F32)
        a_ref[:, cols] = jnp.square(jnp.maximum(u, 0.0)).astype(BF16)
    o_ref[...] = x + jnp.dot(a_ref[...], wd_ref[...], preferred_element_type=F32)


def _mlp(x, g, wu, wd, layer, tm):
    n = x.shape[0]
    return pl.pallas_call(
        _mlp_kernel,
        grid=(n // tm,),
        in_specs=[
            pl.BlockSpec((tm, D_MODEL), lambda i: (i, 0)),
            pl.BlockSpec((1, D_MODEL), lambda i: (0, 0)),
            pl.BlockSpec((None, D_MODEL, D_FF), lambda i: (layer, 0, 0)),
            pl.BlockSpec((None, D_FF, D_MODEL), lambda i: (layer, 0, 0)),
        ],
        out_specs=pl.BlockSpec((tm, D_MODEL), lambda i: (i, 0)),
        out_shape=jax.ShapeDtypeStruct((n, D_MODEL), F32),
        scratch_shapes=[pltpu.VMEM((tm, D_FF), BF16)],
        compiler_params=_params("arbitrary"),
        name="mlp",
    )(x, g, wu, wd)


def _pool_prompt_kernel(x_ref, g_ref, w_ref, sc_ref, o_ref, st_ref, ext_ref, *, tm):
    s = pl.program_id(1)
    halo = POOL_STATE + 1

    @pl.when(s == 0)
    def _():
        ext_ref[0:halo, :] = jnp.zeros((halo, D_MODEL), F32)

    x = x_ref[0]
    xn = _rmsnorm(x, g_ref[...])
    ext_ref[halo:halo + tm, :] = xn
    pos = s * tm + lax.broadcasted_iota(jnp.int32, (tm, 1), 0)
    for gi, w in enumerate(POOL_WINDOWS):
        cols = slice(gi * POOL_GC, (gi + 1) * POOL_GC)
        cur = xn[:, cols]
        acc = cur
        for k in range(1, w):
            acc = acc + ext_ref[halo - k:halo - k + tm, cols]
        cnt = jnp.minimum(pos + 1, w).astype(F32)
        d = acc / cnt - cur
        y = jnp.dot(d.astype(BF16), w_ref[gi], preferred_element_type=F32)
        o_ref[0, :, cols] = x[:, cols] + y * sc_ref[:, cols]
    ext_ref[0:halo, :] = ext_ref[tm:tm + halo, :]

    @pl.when(s == pl.num_programs(1) - 1)
    def _():
        st_ref[0] = ext_ref[tm + 1:tm + halo, :]


def _pool_prompt(x, g, w, sc, tm):
    b, s, _ = x.shape
    return pl.pallas_call(
        functools.partial(_pool_prompt_kernel, tm=tm),
        grid=(b, s // tm),
        in_specs=[
            pl.BlockSpec((1, tm, D_MODEL), lambda i, j: (i, j, 0)),
            pl.BlockSpec((1, D_MODEL), lambda i, j: (0, 0)),
            pl.BlockSpec((len(POOL_WINDOWS), POOL_GC, POOL_GC), lambda i, j: (0, 0, 0)),
            pl.BlockSpec((1, D_MODEL), lambda i, j: (0, 0)),
        ],
        out_specs=[
            pl.BlockSpec((1, tm, D_MODEL), lambda i, j: (i, j, 0)),
            pl.BlockSpec((1, POOL_STATE, D_MODEL), lambda i, j: (i, 0, 0)),
        ],
        out_shape=[
            jax.ShapeDtypeStruct((b, s, D_MODEL), F32),
            jax.ShapeDtypeStruct((b, POOL_STATE, D_MODEL), F32),
        ],
        scratch_shapes=[pltpu.VMEM((tm + POOL_STATE + 1, D_MODEL), F32)],
        compiler_params=_params("arbitrary", "arbitrary"),
        name="pool_prompt",
    )(x, g, w, sc)


def _pool_sample_kernel(x_ref, st_ref, g_ref, w_ref, sc_ref, o_ref, nst_ref, *, pos):
    x = x_ref[...]
    xn = _rmsnorm(x, g_ref[...])
    for gi, w in enumerate(POOL_WINDOWS):
        cols = slice(gi * POOL_GC, (gi + 1) * POOL_GC)
        cur = xn[:, cols]
        acc = cur
        for k in range(1, w):
            acc = acc + st_ref[POOL_STATE - k, :, cols]
        d = acc / float(min(pos + 1, w)) - cur
        y = jnp.dot(d.astype(BF16), w_ref[gi], preferred_element_type=F32)
        o_ref[:, cols] = x[:, cols] + y * sc_ref[:, cols]
    for k in range(POOL_STATE - 1):
        nst_ref[k] = st_ref[k + 1]
    nst_ref[POOL_STATE - 1] = xn


def _pool_sample(x, st, g, w, sc, pos):
    n = x.shape[0]
    return pl.pallas_call(
        functools.partial(_pool_sample_kernel, pos=pos),
        out_shape=[
            jax.ShapeDtypeStruct((n, D_MODEL), F32),
            jax.ShapeDtypeStruct((POOL_STATE, n, D_MODEL), F32),
        ],
        compiler_params=pltpu.CompilerParams(vmem_limit_bytes=VMEM_LIMIT),
        name="pool_sample",
    )(x, st, g, w, sc)


def _rope_tables(pos):
    half = HEAD_DIM // 2
    inv = ROPE_THETA ** (-jnp.arange(half, dtype=F32) * 2.0 / HEAD_DIM)
    ang = pos.astype(F32)[:, None] * inv[None, :]
    cos, sin = jnp.cos(ang), jnp.sin(ang)
    return (jnp.concatenate([cos, cos, cos, cos], axis=1),
            jnp.concatenate([-sin, sin, -sin, sin], axis=1))


def _gain_tables(gain):
    half = HEAD_DIM // 2
    swapped = jnp.concatenate([gain[:, half:], gain[:, :half]], axis=1)
    return jnp.concatenate([gain, gain], axis=1), jnp.concatenate([swapped, swapped], axis=1)


def _qkv_kernel(x_ref, g_ref, w_ref, gq_ref, gqs_ref, gk_ref, gks_ref, cos_ref, sin_ref, ones_ref,
                *rest, tm, dils, out_dtype, with_tail):
    outs = rest[:9]
    tail_ref = rest[9] if with_tail else None
    stage_ref = rest[-1]
    x = x_ref[0]
    h = _rmsnorm(x, g_ref[...]).astype(BF16)
    cosf = cos_ref[...]
    sinf = sin_ref[...]
    lane = lax.broadcasted_iota(jnp.int32, (tm, LANES), 1)
    first_half = (lane % HEAD_DIM) < (HEAD_DIM // 2)
    for part in range(3):
        for gi in range(N_GROUPS):
            c0 = (part * N_GROUPS + gi) * GROUP_W
            res = jnp.dot(h, w_ref[:, c0:c0 + GROUP_W], preferred_element_type=F32)
            if part < 2:
                gain = (gq_ref, gk_ref)[part][gi:gi + 1, :]
                gain_sw = (gqs_ref, gks_ref)[part][gi:gi + 1, :]
                ta = gain * cosf
                tb = gain_sw * sinf
                if part == 0:
                    ta = ta * (HEAD_DIM ** -0.5)
                    tb = tb * (HEAD_DIM ** -0.5)
                sq = (res * res).astype(BF16)
                pieces = []
                for hh in range(GROUP_W // MXU_DIM):
                    ss = jnp.dot(sq[:, hh * MXU_DIM:(hh + 1) * MXU_DIM], ones_ref[...],
                                 preferred_element_type=F32)
                    r = lax.rsqrt(ss * (1.0 / HEAD_DIM) + EPS)
                    for sl in range(MXU_DIM // LANES):
                        l0 = hh * MXU_DIM + sl * LANES
                        xs = res[:, l0:l0 + LANES]
                        swapped = jnp.where(first_half,
                                            pltpu.roll(xs, LANES - HEAD_DIM // 2, 1),
                                            pltpu.roll(xs, HEAD_DIM // 2, 1))
                        pieces.append(r[:, sl * LANES:(sl + 1) * LANES] * (xs * ta + swapped * tb))
                res = jnp.concatenate(pieces, axis=1)
            if with_tail and part > 0:
                t0 = (gi * 2 + part - 1) * GROUP_W
                tail_ref[0, :, t0:t0 + GROUP_W] = res
            out_ref = outs[part * N_GROUPS + gi]
            d = dils[gi]
            if d == 1:
                out_ref[0, 0] = res.astype(out_dtype)
            else:
                for c in range(GROUP_W // LANES):
                    stage_ref[c] = res[:, c * LANES:(c + 1) * LANES]
                for r_ in range(d):
                    for c in range(GROUP_W // LANES):
                        out_ref[0, r_, :, c * LANES:(c + 1) * LANES] = (
                            stage_ref[c, pl.ds(r_, tm // d, stride=d), :].astype(out_dtype))


def _qkv(x, g, w, li, gq, gk, cosf, sinf, ones, *, tm, dils, out_dtype, tail_rows):
    b, s, _ = x.shape
    nt = s // tm
    with_tail = tail_rows > 0
    gq_t, gq_s = gq
    gk_t, gk_s = gk
    const2 = lambda i, j: (0, 0)
    in_specs = [
        pl.BlockSpec((1, tm, D_MODEL), lambda i, j: (i, j, 0)),
        pl.BlockSpec((1, D_MODEL), const2),
        pl.BlockSpec((None, D_MODEL, QKV_W), lambda i, j: (li, 0, 0)),
        pl.BlockSpec((N_GROUPS, LANES), const2),
        pl.BlockSpec((N_GROUPS, LANES), const2),
        pl.BlockSpec((N_GROUPS, LANES), const2),
        pl.BlockSpec((N_GROUPS, LANES), const2),
        pl.BlockSpec((tm, LANES), lambda i, j: (j, 0)),
        pl.BlockSpec((tm, LANES), lambda i, j: (j, 0)),
        pl.BlockSpec((MXU_DIM, MXU_DIM), const2),
    ]
    out_specs, out_shape = [], []
    for _ in range(3):
        for d in dils:
            out_specs.append(pl.BlockSpec((1, d, tm // d, GROUP_W), lambda i, j: (i, 0, j, 0)))
            out_shape.append(jax.ShapeDtypeStruct((b, d, s // d, GROUP_W), out_dtype))
    if with_tail:
        t0 = nt - tail_rows // tm
        out_specs.append(pl.BlockSpec((1, tm, 2 * N_GROUPS * GROUP_W),
                                      lambda i, j: (i, jnp.maximum(j - t0, 0), 0)))
        out_shape.append(jax.ShapeDtypeStruct((b, tail_rows, 2 * N_GROUPS * GROUP_W), F32))
    return pl.pallas_call(
        functools.partial(_qkv_kernel, tm=tm, dils=dils, out_dtype=out_dtype, with_tail=with_tail),
        grid=(b, nt),
        in_specs=in_specs,
        out_specs=out_specs,
        out_shape=out_shape,
        scratch_shapes=[pltpu.VMEM((GROUP_W // LANES, tm, LANES), F32)],
        compiler_params=_params("arbitrary", "arbitrary"),
        name="qkv",
    )(x, g, w, gq_t, gq_s, gk_t, gk_s, cosf, sinf, ones)


def _attn_kernel(q_ref, kp_ref, kc_ref, vp_ref, vc_ref, o_ref, lse_ref, kcat_ref, vcat_ref, *, tq):
    j = pl.program_id(1)
    kcat_ref[0:Q_SUB, :] = kp_ref[0]
    kcat_ref[Q_SUB:Q_SUB + tq, :] = kc_ref[0]
    vcat_ref[0:Q_SUB, :] = vp_ref[0]
    vcat_ref[Q_SUB:Q_SUB + tq, :] = vc_ref[0]

    nk = 2 * Q_SUB
    row = lax.broadcasted_iota(jnp.int32, (Q_SUB, nk), 0)
    col = lax.broadcasted_iota(jnp.int32, (Q_SUB, nk), 1)
    dist = row + Q_SUB - col
    band = (dist >= 0) & (dist <= N_KEYS_M1)
    lane = lax.broadcasted_iota(jnp.int32, (Q_SUB, LANES), 1)
    low_head = lane < HEAD_DIM
    zero_q = jnp.zeros((Q_SUB, LANES), BF16)

    def sub_block(sb, carry):
        r0 = pl.multiple_of(sb * Q_SUB, Q_SUB)
        valid = band & ((col >= Q_SUB) | (sb + j > 0))
        for p in range(GROUP_W // LANES):
            lanes = slice(p * LANES, (p + 1) * LANES)
            qp = q_ref[0, pl.ds(r0, Q_SUB), lanes]
            kp = kcat_ref[pl.ds(r0, nk), lanes]
            vp = vcat_ref[pl.ds(r0, nk), lanes]
            o_pair = None
            lse_pair = None
            for hh in range(2):
                sel = low_head if hh == 0 else jnp.logical_not(low_head)
                qm = jnp.where(sel, qp, zero_q)
                s = lax.dot_general(qm, kp, (((1,), (1,)), ((), ())), preferred_element_type=F32)
                s = jnp.where(valid, s, NEG)
                m = jnp.max(s, axis=1, keepdims=True)
                pr = jnp.exp(s - m)
                l = jnp.sum(pr, axis=1, keepdims=True)
                o = jnp.dot(pr.astype(BF16), vp, preferred_element_type=F32) / l
                lse = jnp.broadcast_to(m + jnp.log(l), (Q_SUB, LANES))
                if hh == 0:
                    o_pair, lse_pair = o, lse
                else:
                    o_pair = jnp.where(low_head, o_pair, o)
                    lse_pair = jnp.where(low_head, lse_pair, lse)
            o_ref[0, pl.ds(r0, Q_SUB), lanes] = o_pair
            lse_ref[0, pl.ds(r0, Q_SUB), lanes] = lse_pair
        return carry

    lax.fori_loop(0, tq // Q_SUB, sub_block, 0)


def _attn(q, k, v, tq):
    nseq, length, _ = q.shape
    ratio = tq // Q_SUB
    cur = lambda i, j: (i, j, 0)
    prev = lambda i, j: (i, jnp.maximum(j * ratio - 1, 0), 0)
    return pl.pallas_call(
        functools.partial(_attn_kernel, tq=tq),
        grid=(nseq, length // tq),
        in_specs=[
            pl.BlockSpec((1, tq, GROUP_W), cur),
            pl.BlockSpec((1, Q_SUB, GROUP_W), prev),
            pl.BlockSpec((1, tq, GROUP_W), cur),
            pl.BlockSpec((1, Q_SUB, GROUP_W), prev),
            pl.BlockSpec((1, tq, GROUP_W), cur),
        ],
        out_specs=[pl.BlockSpec((1, tq, GROUP_W), cur), pl.BlockSpec((1, tq, GROUP_W), cur)],
        out_shape=[jax.ShapeDtypeStruct((nseq, length, GROUP_W), F32)] * 2,
        scratch_shapes=[pltpu.VMEM((tq + Q_SUB, GROUP_W), BF16)] * 2,
        compiler_params=_params("arbitrary", "arbitrary"),
        name="attn_prompt",
    )(q, k, k, v, v)


def _merge_kernel(x_ref, o0, o1, o2, l0, l1, l2, wo_ref, y_ref, *scratch, tm, dils):
    def rows(ref, d, scr):
        if d == 1:
            return ref[0, 0]
        nc = GROUP_W // LANES
        for r_ in range(d):
            for c in range(nc):
                scr[c, pl.ds(r_, tm // d, stride=d), :] = ref[0, r_, :, c * LANES:(c + 1) * LANES]
        return jnp.concatenate([scr[c] for c in range(nc)], axis=1)

    o_refs, l_refs = (o0, o1, o2), (l0, l1, l2)
    os_ = [rows(o_refs[g], dils[g], scratch[2 * g]) for g in range(N_GROUPS)]
    ls_ = [rows(l_refs[g], dils[g], scratch[2 * g + 1]) for g in range(N_GROUPS)]
    mx = jnp.maximum(jnp.maximum(ls_[0], ls_[1]), ls_[2])
    ws = [jnp.exp(l - mx) for l in ls_]
    num = ws[0] * os_[0] + ws[1] * os_[1] + ws[2] * os_[2]
    den = ws[0] + ws[1] + ws[2]
    merged = (num / den).astype(BF16)
    y_ref[0] = x_ref[0] + jnp.dot(merged, wo_ref[...], preferred_element_type=F32)


def _merge(x, os_, ls_, wo, li, *, tm, dils):
    b, s, _ = x.shape
    grp = [pl.BlockSpec((1, d, tm // d, GROUP_W), lambda i, j: (i, 0, j, 0)) for d in dils]
    return pl.pallas_call(
        functools.partial(_merge_kernel, tm=tm, dils=dils),
        grid=(b, s // tm),
        in_specs=[pl.BlockSpec((1, tm, D_MODEL), lambda i, j: (i, j, 0))] + grp + grp
        + [pl.BlockSpec((None, GROUP_W, D_MODEL), lambda i, j: (li, 0, 0))],
        out_specs=pl.BlockSpec((1, tm, D_MODEL), lambda i, j: (i, j, 0)),
        out_shape=jax.ShapeDtypeStruct((b, s, D_MODEL), F32),
        scratch_shapes=[pltpu.VMEM((GROUP_W // LANES, tm, LANES), F32)] * (2 * N_GROUPS),
        compiler_params=_params("arbitrary", "arbitrary"),
        name="merge_out_proj",
    )(x, *os_, *ls_, wo)


def _cache_kernel(c_ref, q_ref, kn_ref, vn_ref, kvt_ref, *rest, n, dil, aliased):
    co_ref, o_ref, lse_ref = rest[1:] if aliased else rest
    b = pl.program_id(0)
    hw = GROUP_W // 2
    nh = hw // HEAD_DIM
    kt = c_ref[0, 0, 0]
    vt = c_ref[0, 0, 1]
    sub = lax.broadcasted_iota(jnp.int32, (nh, hw), 0)
    ln = lax.broadcasted_iota(jnp.int32, (nh, hw), 1)
    own = (ln // HEAD_DIM) == sub
    qb = jnp.where(own, jnp.broadcast_to(q_ref[0], (nh, hw)), 0.0)
    s = jnp.dot(qb.astype(BF16), kt.astype(BF16), preferred_element_type=F32)
    e = lax.broadcasted_iota(jnp.int32, (nh, n), 1)
    s = jnp.where((n - e) % dil == 0, s, NEG)
    s_new = jnp.sum(qb * kn_ref[0], axis=1, keepdims=True)
    m = jnp.maximum(jnp.max(s, axis=1, keepdims=True), s_new)
    pr = jnp.exp(s - m)
    p_new = jnp.exp(s_new - m)
    l = jnp.sum(pr, axis=1, keepdims=True) + p_new
    o = lax.dot_general(pr.astype(BF16), vt.astype(BF16), (((1,), (1,)), ((), ())),
                        preferred_element_type=F32)
    o = (o + p_new * vn_ref[0]) / l
    o_ref[0] = jnp.sum(jnp.where(own, o, 0.0), axis=0, keepdims=True)
    lse_ref[0] = jnp.sum(jnp.where(own, m + jnp.log(l), 0.0), axis=0, keepdims=True)

    bl = lax.broadcasted_iota(jnp.int32, kvt_ref.shape[2:], 1)
    en = lax.broadcasted_iota(jnp.int32, (hw, n), 1)
    for part, old in enumerate((kt, vt)):
        new_col = jnp.sum(jnp.where(bl == b, kvt_ref[0, part], 0.0), axis=1, keepdims=True)
        co_ref[0, 0, part] = jnp.where(en == n - 1, new_col, pltpu.roll(old, n - 1, 1))


def _cache_step(cache, layer, prev_out, q, kn, vn, kvt, gi, n, dil):
    nl, nb = cache.shape[:2]
    hw = GROUP_W // 2
    aliased = prev_out is not None
    cblk = pl.BlockSpec((1, 1, 2, hw, n), lambda i, j: (layer, i, 0, j, 0))
    row = pl.BlockSpec((1, 1, hw), lambda i, j: (i, 0, 2 * gi + j))
    in_specs = [cblk, row, row, row, pl.BlockSpec((1, 2, hw, nb), lambda i, j: (gi, 0, j, 0))]
    args = [cache, q, kn, vn, kvt]
    if aliased:
        in_specs.append(pl.BlockSpec(memory_space=pl.ANY))
        args.append(prev_out)
    orow = pl.BlockSpec((1, 1, hw), lambda i, j: (i, 0, j))
    return pl.pallas_call(
        functools.partial(_cache_kernel, n=n, dil=dil, aliased=aliased),
        grid=(nb, 2),
        in_specs=in_specs,
        out_specs=[cblk, orow, orow],
        out_shape=[
            jax.ShapeDtypeStruct(cache.shape, F32),
            jax.ShapeDtypeStruct((nb, 1, GROUP_W), F32),
            jax.ShapeDtypeStruct((nb, 1, GROUP_W), F32),
        ],
        input_output_aliases={5: 0} if aliased else {},
        compiler_params=_params("arbitrary", "arbitrary"),
        name="cache_step",
    )(*args)


def kernel(x_prompt, x_sample, state_pool, cache_kv_w128, cache_kv_w512, cache_kv_w2048, norm_mix,
           norm_mlp, pool_w, pool_scale, attn_w_qkv, attn_q_norm, attn_k_norm, attn_w_o, mlp_w_up,
           mlp_w_down):
    bsz, seq, _ = x_prompt.shape
    nb = x_sample.shape[0]
    past = PAST_LEN
    dils = tuple(d for _, d in ATTN_PATTERNS)
    ones_sample = (1, 1, 1)

    wu = mlp_w_up.astype(BF16)
    wd = mlp_w_down.astype(BF16)
    wqkv = attn_w_qkv.astype(BF16)
    wo = attn_w_o.astype(BF16)
    wp = pool_w.astype(BF16)
    head = lax.broadcasted_iota(jnp.int32, (MXU_DIM, MXU_DIM), 0) // HEAD_DIM
    ones = (head == head.T).astype(BF16)
    cos_p, sin_p = _rope_tables(jnp.arange(seq))
    cos_s, sin_s = _rope_tables(jnp.full((nb,), past))

    caches = [jnp.transpose(c, (0, 1, 3, 4, 5, 2)).reshape(c.shape[0], nb, 2, GROUP_W, c.shape[2])
              for c in (cache_kv_w128, cache_kv_w512, cache_kv_w2048)]

    xp = x_prompt
    xs = x_sample.reshape(nb, D_MODEL)
    pool_p, pool_s, kv_p = [], [], [[] for _ in range(N_GROUPS)]
    new_caches = [None] * N_GROUPS
    for layer in range(DEPTH):
        li = layer // 2
        g_mix = norm_mix[layer][None, :]
        if layer % 2 == 0:
            sc = pool_scale[li][None, :]
            xp, st = _pool_prompt(xp, g_mix, wp[li], sc, ROW_TILE)
            pool_p.append(st)
            st_in = jnp.transpose(state_pool[li], (1, 0, 2))
            xs, st = _pool_sample(xs, st_in, g_mix, wp[li], sc, past)
            pool_s.append(jnp.transpose(st, (1, 0, 2)))
        else:
            gq = _gain_tables(attn_q_norm[li])
            gk = _gain_tables(attn_k_norm[li])
            outs = _qkv(xp, g_mix, wqkv, li, gq, gk, cos_p, sin_p, ones, tm=ROW_TILE, dils=dils,
                        out_dtype=BF16, tail_rows=TAIL)
            os_, ls_ = [], []
            for gi, (w, d) in enumerate(ATTN_PATTERNS):
                q, k, v = (outs[part * N_GROUPS + gi].reshape(bsz * d, seq // d, GROUP_W)
                           for part in range(3))
                o, lse = _attn(q, k, v, min(ATTN_TQ, seq // d))
                os_.append(o.reshape(bsz, d, seq // d, GROUP_W))
                ls_.append(lse.reshape(bsz, d, seq // d, GROUP_W))
                n_out = min(w, seq)
                kv_p[gi].append(outs[9][:, TAIL - n_out:, gi * 2 * GROUP_W:(gi + 1) * 2 * GROUP_W]
                                .reshape(bsz, n_out, 2, HEADS, HEAD_DIM))
            xp = _merge(xp, os_, ls_, wo, li, tm=ROW_TILE, dils=dils)
            outs = _qkv(xs[None], g_mix, wqkv, li, gq, gk, cos_s, sin_s, ones, tm=nb, dils=ones_sample,
                        out_dtype=F32, tail_rows=0)
            q, kn, vn = (jnp.concatenate([outs[part * N_GROUPS + gi][0, 0] for gi in range(N_GROUPS)],
                                         axis=1)[:, None, :] for part in range(3))
            kvt = jnp.stack([jnp.stack([outs[part * N_GROUPS + gi][0, 0].T for part in (1, 2)])
                             for gi in range(N_GROUPS)])
            os_, ls_ = [], []
            for gi, (w, d) in enumerate(ATTN_PATTERNS):
                n = caches[gi].shape[-1]
                new_caches[gi], o, lse = _cache_step(caches[gi], li, new_caches[gi], q, kn, vn, kvt,
                                                     gi, n, d)
                os_.append(o.reshape(1, 1, nb, GROUP_W))
                ls_.append(lse.reshape(1, 1, nb, GROUP_W))
            xs = _merge(xs[None], os_, ls_, wo, li, tm=nb, dils=ones_sample)[0]
        g_mlp = norm_mlp[layer][None, :]
        xp = _mlp(xp.reshape(bsz * seq, D_MODEL), g_mlp, wu, wd, layer, ROW_TILE)
        xp = xp.reshape(bsz, seq, D_MODEL)
        xs = _mlp(xs, g_mlp, wu, wd, layer, nb)

    kv_s = [jnp.transpose(c.reshape(c.shape[0], nb, 2, HEADS, HEAD_DIM, c.shape[-1]), (0, 1, 5, 2, 3, 4))
            for c in new_caches]
    kv_p = [jnp.stack(t) for t in kv_p]
    return (xp, xs.reshape(nb, 1, D_MODEL), jnp.stack(pool_p), jnp.stack(pool_s),
            kv_p[0], kv_s[0], kv_p[1], kv_s[1], kv_p[2], kv_s[2])
```

```python
import functools

import jax
import jax.numpy as jnp
from jax import lax
from jax.experimental import pallas as pl
from jax.experimental.pallas import tpu as pltpu

F32 = jnp.float32
BF16 = jnp.bfloat16

D_MODEL = 1024
D_FF = 4 * D_MODEL
DEPTH = 4
POOL_WINDOWS = (2, 4, 8, 16)
POOL_GC = D_MODEL // len(POOL_WINDOWS)
POOL_STATE = max(POOL_WINDOWS) - 1
ATTN_PATTERNS = ((128, 1), (512, 4), (2048, 16))
N_GROUPS = len(ATTN_PATTERNS)
HEAD_DIM = 64
HEADS = 8
GROUP_W = HEADS * HEAD_DIM
QKV_W = 3 * N_GROUPS * GROUP_W
N_KEYS_M1 = 128
ROPE_THETA = 10000.0
PAST_LEN = 8192
EPS = 1e-6
NEG = -1e30
LOG2E = 1.4426950408889634
Q_SCALE = HEAD_DIM ** -0.5 * LOG2E

LANES = 128
MXU_DIM = 256
VMEM_LIMIT = 56 * 1024 * 1024
ROW_TILE = 512
FF_CHUNK = 512
MIXER_PIECES = 8
ATTN_TQ = 512
Q_SUB = 128
TAIL = max(w for w, _ in ATTN_PATTERNS)
CACHE_BLOCK_BYTES = 4 * 1024 * 1024


def _params(*sem):
    return pltpu.CompilerParams(dimension_semantics=sem, vmem_limit_bytes=VMEM_LIMIT)


def _rmsnorm(x, g):
    return x * lax.rsqrt(jnp.mean(x * x, axis=-1, keepdims=True) + EPS) * g


def _mlp_residual(x, g, wu_ref, wd_ref, a_ref, side=()):
    h = _rmsnorm(x, g).astype(BF16)
    n_chunks = D_FF // FF_CHUNK
    for c in range(n_chunks):
        cols = slice(c * FF_CHUNK, (c + 1) * FF_CHUNK)
        u = jnp.dot(h, wu_ref[:, cols], preferred_element_type=F32)
        a_ref[:, cols] = jnp.square(jnp.maximum(u, 0.0)).astype(BF16)
        for j, piece in enumerate(side):
            if j * n_chunks // len(side) == c:
                piece()
    return x + jnp.dot(a_ref[...], wd_ref[...], preferred_element_type=F32)


def _mlp_kernel(x_ref, g_ref, wu_ref, wd_ref, o_ref, a_ref):
    o_ref[...] = _mlp_residual(x_ref[...], g_ref[...], wu_ref, wd_ref, a_ref)


def _mlp(x, g, wu, wd, layer, tm):
    n = x.shape[0]
    return pl.pallas_call(
        _mlp_kernel,
        grid=(n // tm,),
        in_specs=[
            pl.BlockSpec((tm, D_MODEL), lambda i: (i, 0)),
            pl.BlockSpec((1, D_MODEL), lambda i: (0, 0)),
            pl.BlockSpec((None, D_MODEL, D_FF), lambda i: (layer, 0, 0)),
            pl.BlockSpec((None, D_FF, D_MODEL), lambda i: (layer, 0, 0)),
        ],
        out_specs=pl.BlockSpec((tm, D_MODEL), lambda i: (i, 0)),
        out_shape=jax.ShapeDtypeStruct((n, D_MODEL), F32),
        scratch_shapes=[pltpu.VMEM((tm, D_FF), BF16)],
        compiler_params=_params("arbitrary"),
        name="mlp",
    )(x, g, wu, wd)


def _skew_specs(n_tiles, tm):
    x_spec = pl.BlockSpec((tm, D_MODEL), lambda i: (jnp.minimum(i, n_tiles - 1), 0))
    o_spec = pl.BlockSpec((tm, D_MODEL), lambda i: (jnp.maximum(i - 1, 0), 0))
    return x_spec, o_spec


def _skew_scratch(tm):
    return [pltpu.VMEM((tm, D_MODEL), F32), pltpu.VMEM((tm, D_MODEL), F32), pltpu.VMEM((tm, D_FF), BF16)]


def _mlp_specs(layer):
    return [pl.BlockSpec((1, D_MODEL), lambda i: (0, 0)),
            pl.BlockSpec((None, D_MODEL, D_FF), lambda i: (layer, 0, 0)),
            pl.BlockSpec((None, D_FF, D_MODEL), lambda i: (layer, 0, 0))]


def _pool_mlp_kernel(x_ref, gmix_ref, w_ref, sc_ref, gmlp_ref, wu_ref, wd_ref, o_ref, st_ref,
                     ext_ref, mid_next, mid_cur, a_ref, *, tm, tiles_per_seq, n_tiles):
    i = pl.program_id(0)
    halo = POOL_STATE + 1

    @pl.when(i == 0)
    def _():
        ext_ref[0:halo, :] = jnp.zeros((halo, D_MODEL), F32)
        mid_cur[...] = jnp.zeros((tm, D_MODEL), F32)

    s = jnp.minimum(i, n_tiles - 1) % tiles_per_seq
    ext_ref[0:halo, :] = jnp.where(s == 0, 0.0, ext_ref[0:halo, :])

    def normalise():
        ext_ref[halo:halo + tm, :] = _rmsnorm(x_ref[...], gmix_ref[...])

    def pool_block(gi, r0, nr):
        w = POOL_WINDOWS[gi]
        cols = slice(gi * POOL_GC, (gi + 1) * POOL_GC)
        pos = s * tm + r0 + lax.broadcasted_iota(jnp.int32, (nr, 1), 0)
        cur = ext_ref[halo + r0:halo + r0 + nr, cols]
        acc = cur
        for k in range(1, w):
            acc = acc + ext_ref[halo + r0 - k:halo + r0 - k + nr, cols]
        cnt = jnp.minimum(pos + 1, w).astype(F32)
        d = (acc / cnt - cur).astype(BF16)
        y = jnp.dot(d, w_ref[gi], preferred_element_type=F32)
        mid_next[r0:r0 + nr, cols] = x_ref[r0:r0 + nr, cols] + y * sc_ref[:, cols]

    side = [normalise]
    for gi, splits in ((3, 4), (2, 2), (1, 1), (0, 1)):
        nr = tm // splits
        side += [functools.partial(pool_block, gi, p * nr, nr) for p in range(splits)]
    o_ref[...] = _mlp_residual(mid_cur[...], gmlp_ref[...], wu_ref, wd_ref, a_ref, side)

    st_ref[0] = ext_ref[tm + 1:tm + halo, :]
    ext_ref[0:halo, :] = ext_ref[tm:tm + halo, :]
    mid_cur[...] = mid_next[...]


def _pool_mlp(x, gmix, w, sc, gmlp, wu, wd, layer, tm):
    b, s, _ = x.shape
    tps = s // tm
    n_tiles = b * tps
    x_spec, o_spec = _skew_specs(n_tiles, tm)
    const = lambda i: (0, 0)
    y, st = pl.pallas_call(
        functools.partial(_pool_mlp_kernel, tm=tm, tiles_per_seq=tps, n_tiles=n_tiles),
        grid=(n_tiles + 1,),
        in_specs=[
            x_spec,
            pl.BlockSpec((1, D_MODEL), const),
            pl.BlockSpec((len(POOL_WINDOWS), POOL_GC, POOL_GC), lambda i: (0, 0, 0)),
            pl.BlockSpec((1, D_MODEL), const),
        ] + _mlp_specs(layer),
        out_specs=[
            o_spec,
            pl.BlockSpec((1, POOL_STATE, D_MODEL),
                         lambda i: (jnp.minimum(i, n_tiles - 1) // tps, 0, 0)),
        ],
        out_shape=[
            jax.ShapeDtypeStruct((b * s, D_MODEL), F32),
            jax.ShapeDtypeStruct((b, POOL_STATE, D_MODEL), F32),
        ],
        scratch_shapes=[pltpu.VMEM((tm + POOL_STATE + 1, D_MODEL), F32)] + _skew_scratch(tm),
        compiler_params=_params("arbitrary"),
        name="pool_mlp",
    )(x.reshape(b * s, D_MODEL), gmix, w, sc, gmlp, wu, wd)
    return y.reshape(b, s, D_MODEL), st


def _pool_sample_kernel(x_ref, st_ref, g_ref, w_ref, sc_ref, o_ref, nst_ref, *, pos):
    x = x_ref[...]
    xn = _rmsnorm(x, g_ref[...])
    for gi, w in enumerate(POOL_WINDOWS):
        cols = slice(gi * POOL_GC, (gi + 1) * POOL_GC)
        cur = xn[:, cols]
        acc = cur
        for k in range(1, w):
            acc = acc + st_ref[POOL_STATE - k, :, cols]
        d = acc / float(min(pos + 1, w)) - cur
        y = jnp.dot(d.astype(BF16), w_ref[gi], preferred_element_type=F32)
        o_ref[:, cols] = x[:, cols] + y * sc_ref[:, cols]
    for k in range(POOL_STATE - 1):
        nst_ref[k] = st_ref[k + 1]
    nst_ref[POOL_STATE - 1] = xn


def _pool_sample(x, st, g, w, sc, pos):
    n = x.shape[0]
    return pl.pallas_call(
        functools.partial(_pool_sample_kernel, pos=pos),
        out_shape=[
            jax.ShapeDtypeStruct((n, D_MODEL), F32),
            jax.ShapeDtypeStruct((POOL_STATE, n, D_MODEL), F32),
        ],
        compiler_params=pltpu.CompilerParams(vmem_limit_bytes=VMEM_LIMIT),
        name="pool_sample",
    )(x, st, g, w, sc)


def _rope_tables(pos):
    half = HEAD_DIM // 2
    inv = ROPE_THETA ** (-jnp.arange(half, dtype=F32) * 2.0 / HEAD_DIM)
    ang = pos.astype(F32)[:, None] * inv[None, :]
    cos, sin = jnp.cos(ang), jnp.sin(ang)
    return (jnp.concatenate([cos, cos, cos, cos], axis=1),
            jnp.concatenate([-sin, sin, -sin, sin], axis=1))


def _gain_tables(gain):
    half = HEAD_DIM // 2
    swapped = jnp.concatenate([gain[:, half:], gain[:, :half]], axis=1)
    return jnp.concatenate([gain, gain], axis=1), jnp.concatenate([swapped, swapped], axis=1)


def _qkv_kernel(x_ref, g_ref, w_ref, gq_ref, gqs_ref, gk_ref, gks_ref, cos_ref, sin_ref, ones_ref,
                *rest, tm, dils, out_dtype, with_tail):
    outs = rest[:9]
    tail_ref = rest[9] if with_tail else None
    stage_ref = rest[-1]
    x = x_ref[0]
    h = _rmsnorm(x, g_ref[...]).astype(BF16)
    cosf = cos_ref[...]
    sinf = sin_ref[...]
    lane = lax.broadcasted_iota(jnp.int32, (tm, LANES), 1)
    first_half = (lane % HEAD_DIM) < (HEAD_DIM // 2)
    for part in range(3):
        for gi in range(N_GROUPS):
            c0 = (part * N_GROUPS + gi) * GROUP_W
            res = jnp.dot(h, w_ref[:, c0:c0 + GROUP_W], preferred_element_type=F32)
            if part < 2:
                gain = (gq_ref, gk_ref)[part][gi:gi + 1, :]
                gain_sw = (gqs_ref, gks_ref)[part][gi:gi + 1, :]
                ta = gain * cosf
                tb = gain_sw * sinf
                if part == 0:
                    ta = ta * Q_SCALE
                    tb = tb * Q_SCALE
                sq = (res * res).astype(BF16)
                pieces = []
                for hh in range(GROUP_W // MXU_DIM):
                    ss = jnp.dot(sq[:, hh * MXU_DIM:(hh + 1) * MXU_DIM], ones_ref[...],
                                 preferred_element_type=F32)
                    r = lax.rsqrt(ss * (1.0 / HEAD_DIM) + EPS)
                    for sl in range(MXU_DIM // LANES):
                        l0 = hh * MXU_DIM + sl * LANES
                        xs = res[:, l0:l0 + LANES]
                        swapped = jnp.where(first_half,
                                            pltpu.roll(xs, LANES - HEAD_DIM // 2, 1),
                                            pltpu.roll(xs, HEAD_DIM // 2, 1))
                        pieces.append(r[:, sl * LANES:(sl + 1) * LANES] * (xs * ta + swapped * tb))
                res = jnp.concatenate(pieces, axis=1)
            if with_tail and part > 0:
                t0 = (gi * 2 + part - 1) * GROUP_W
                tail_ref[0, :, t0:t0 + GROUP_W] = res
            out_ref = outs[part * N_GROUPS + gi]
            d = dils[gi]
            if d == 1:
                out_ref[0, 0] = res.astype(out_dtype)
            else:
                for c in range(GROUP_W // LANES):
                    stage_ref[c] = res[:, c * LANES:(c + 1) * LANES]
                for r_ in range(d):
                    for c in range(GROUP_W // LANES):
                        out_ref[0, r_, :, c * LANES:(c + 1) * LANES] = (
                            stage_ref[c, pl.ds(r_, tm // d, stride=d), :].astype(out_dtype))


def _qkv(x, g, w, li, gq, gk, cosf, sinf, ones, *, tm, dils, out_dtype, tail_rows):
    b, s, _ = x.shape
    nt = s // tm
    with_tail = tail_rows > 0
    gq_t, gq_s = gq
    gk_t, gk_s = gk
    const2 = lambda i, j: (0, 0)
    in_specs = [
        pl.BlockSpec((1, tm, D_MODEL), lambda i, j: (i, j, 0)),
        pl.BlockSpec((1, D_MODEL), const2),
        pl.BlockSpec((None, D_MODEL, QKV_W), lambda i, j: (li, 0, 0)),
        pl.BlockSpec((N_GROUPS, LANES), const2),
        pl.BlockSpec((N_GROUPS, LANES), const2),
        pl.BlockSpec((N_GROUPS, LANES), const2),
        pl.BlockSpec((N_GROUPS, LANES), const2),
        pl.BlockSpec((tm, LANES), lambda i, j: (j, 0)),
        pl.BlockSpec((tm, LANES), lambda i, j: (j, 0)),
        pl.BlockSpec((MXU_DIM, MXU_DIM), const2),
    ]
    out_specs, out_shape = [], []
    for _ in range(3):
        for d in dils:
            out_specs.append(pl.BlockSpec((1, d, tm // d, GROUP_W), lambda i, j: (i, 0, j, 0)))
            out_shape.append(jax.ShapeDtypeStruct((b, d, s // d, GROUP_W), out_dtype))
    if with_tail:
        t0 = nt - tail_rows // tm
        out_specs.append(pl.BlockSpec((1, tm, 2 * N_GROUPS * GROUP_W),
                                      lambda i, j: (i, jnp.maximum(j - t0, 0), 0)))
        out_shape.append(jax.ShapeDtypeStruct((b, tail_rows, 2 * N_GROUPS * GROUP_W), F32))
    return pl.pallas_call(
        functools.partial(_qkv_kernel, tm=tm, dils=dils, out_dtype=out_dtype, with_tail=with_tail),
        grid=(b, nt),
        in_specs=in_specs,
        out_specs=out_specs,
        out_shape=out_shape,
        scratch_shapes=[pltpu.VMEM((GROUP_W // LANES, tm, LANES), F32)],
        compiler_params=_params("arbitrary", "arbitrary"),
        name="qkv",
    )(x, g, w, gq_t, gq_s, gk_t, gk_s, cosf, sinf, ones)


def _attn_kernel(q_ref, kp_ref, kc_ref, vp_ref, vc_ref, o_ref, lse_ref, kcat_ref, vcat_ref, *, tq):
    j = pl.program_id(1)
    kcat_ref[0:Q_SUB, :] = kp_ref[0]
    kcat_ref[Q_SUB:Q_SUB + tq, :] = kc_ref[0]
    vcat_ref[0:Q_SUB, :] = vp_ref[0]
    vcat_ref[Q_SUB:Q_SUB + tq, :] = vc_ref[0]

    nk = 2 * Q_SUB
    nq = 2 * Q_SUB
    row = lax.broadcasted_iota(jnp.int32, (nq, nk), 0) & (Q_SUB - 1)
    col = lax.broadcasted_iota(jnp.int32, (nq, nk), 1)
    dist = row + Q_SUB - col
    band = (dist >= 0) & (dist <= N_KEYS_M1)
    lane = lax.broadcasted_iota(jnp.int32, (Q_SUB, LANES), 1)
    low_head = lane < HEAD_DIM
    zero_q = jnp.zeros((Q_SUB, LANES), BF16)
    ones_v = jnp.ones((nk, LANES), BF16)

    def sub_block(sb, carry):
        r0 = pl.multiple_of(sb * Q_SUB, Q_SUB)
        valid = band & ((col >= Q_SUB) | (sb + j > 0))
        for p in range(GROUP_W // LANES):
            lanes = slice(p * LANES, (p + 1) * LANES)
            qp = q_ref[0, pl.ds(r0, Q_SUB), lanes]
            kp = kcat_ref[pl.ds(r0, nk), lanes]
            vaug = jnp.concatenate([vcat_ref[pl.ds(r0, nk), lanes], ones_v], axis=1)
            q2 = jnp.concatenate([jnp.where(low_head, qp, zero_q), jnp.where(low_head, zero_q, qp)], axis=0)
            s = lax.dot_general(q2, kp, (((1,), (1,)), ((), ())), preferred_element_type=F32)
            s = jnp.where(valid, s, NEG)
            m = jnp.max(s, axis=1, keepdims=True)
            pr = jnp.exp2(s - m).astype(BF16)
            ov = jnp.dot(pr, vaug, preferred_element_type=F32)
            den = ov[:, LANES:]
            on = ov[:, :LANES] / den
            lse = m + jnp.log2(den)
            o_ref[0, pl.ds(r0, Q_SUB), lanes] = jnp.where(low_head, on[:Q_SUB], on[Q_SUB:])
            lse_ref[0, pl.ds(r0, Q_SUB), lanes] = jnp.where(low_head, lse[:Q_SUB], lse[Q_SUB:])
        return carry

    lax.fori_loop(0, tq // Q_SUB, sub_block, 0)


def _attn(q, k, v, tq):
    nseq, length, _ = q.shape
    ratio = tq // Q_SUB
    cur = lambda i, j: (i, j, 0)
    prev = lambda i, j: (i, jnp.maximum(j * ratio - 1, 0), 0)
    return pl.pallas_call(
        functools.partial(_attn_kernel, tq=tq),
        grid=(nseq, length // tq),
        in_specs=[
            pl.BlockSpec((1, tq, GROUP_W), cur),
            pl.BlockSpec((1, Q_SUB, GROUP_W), prev),
            pl.BlockSpec((1, tq, GROUP_W), cur),
            pl.BlockSpec((1, Q_SUB, GROUP_W), prev),
            pl.BlockSpec((1, tq, GROUP_W), cur),
        ],
        out_specs=[pl.BlockSpec((1, tq, GROUP_W), cur), pl.BlockSpec((1, tq, GROUP_W), cur)],
        out_shape=[jax.ShapeDtypeStruct((nseq, length, GROUP_W), F32)] * 2,
        scratch_shapes=[pltpu.VMEM((tq + Q_SUB, GROUP_W), BF16)] * 2,
        compiler_params=_params("arbitrary", "arbitrary"),
        name="attn_prompt",
    )(q, k, k, v, v)


def _merge_pieces(load_x, store_y, o_refs, l_refs, wo_ref, und, tm, dils):
    nc = GROUP_W // LANES

    def rows(ref, d, scr, r0, nr):
        if d == 1:
            return ref[0, 0, r0:r0 + nr, :]
        for r_ in range(d):
            for c in range(nc):
                scr[c, pl.ds(r0 + r_, nr // d, stride=d), :] = (
                    ref[0, r_, r0 // d:(r0 + nr) // d, c * LANES:(c + 1) * LANES])
        return jnp.concatenate([scr[c, r0:r0 + nr, :] for c in range(nc)], axis=1)

    def merge_rows(r0, nr):
        os_ = [rows(o_refs[g], dils[g], und[2 * g], r0, nr) for g in range(N_GROUPS)]
        ls_ = [rows(l_refs[g], dils[g], und[2 * g + 1], r0, nr) for g in range(N_GROUPS)]
        mx = jnp.maximum(jnp.maximum(ls_[0], ls_[1]), ls_[2])
        ws = [jnp.exp2(l - mx) for l in ls_]
        num = ws[0] * os_[0] + ws[1] * os_[1] + ws[2] * os_[2]
        den = ws[0] + ws[1] + ws[2]
        merged = (num / den).astype(BF16)
        store_y(r0, nr, load_x(r0, nr) + jnp.dot(merged, wo_ref[...], preferred_element_type=F32))

    nr = min(tm, Q_SUB)
    return [functools.partial(merge_rows, r0, nr) for r0 in range(0, tm, nr)]


def _merge_scratch(tm):
    return [pltpu.VMEM((GROUP_W // LANES, tm, LANES), F32)] * (2 * N_GROUPS)


def _merge_kernel(x_ref, o0, o1, o2, l0, l1, l2, wo_ref, y_ref, *scratch, tm, dils):
    def store_y(r0, nr, v):
        y_ref[0, r0:r0 + nr, :] = v

    for piece in _merge_pieces(lambda r0, nr: x_ref[0, r0:r0 + nr, :], store_y, (o0, o1, o2),
                               (l0, l1, l2), wo_ref, scratch, tm, dils):
        piece()


def _merge_mlp_kernel(x_ref, o0, o1, o2, l0, l1, l2, wo_ref, gmlp_ref, wu_ref, wd_ref, y_ref,
                      *scratch, tm, dils):
    und = scratch[:2 * N_GROUPS]
    mid_next, mid_cur, a_ref = scratch[2 * N_GROUPS:]

    @pl.when(pl.program_id(0) == 0)
    def _():
        mid_cur[...] = jnp.zeros((tm, D_MODEL), F32)

    def store_mid(r0, nr, v):
        mid_next[r0:r0 + nr, :] = v

    side = _merge_pieces(lambda r0, nr: x_ref[r0:r0 + nr, :], store_mid, (o0, o1, o2), (l0, l1, l2),
                         wo_ref, und, tm, dils)
    y_ref[...] = _mlp_residual(mid_cur[...], gmlp_ref[...], wu_ref, wd_ref, a_ref, side)
    mid_cur[...] = mid_next[...]


def _merge_mlp(x, os_, ls_, wo, li, gmlp, wu, wd, layer, *, tm, dils):
    b, s, _ = x.shape
    tps = s // tm
    n_tiles = b * tps
    x_spec, o_spec = _skew_specs(n_tiles, tm)

    def grp_map(i):
        t = jnp.minimum(i, n_tiles - 1)
        return (t // tps, 0, t % tps, 0)

    grp = [pl.BlockSpec((1, d, tm // d, GROUP_W), grp_map) for d in dils]
    y = pl.pallas_call(
        functools.partial(_merge_mlp_kernel, tm=tm, dils=dils),
        grid=(n_tiles + 1,),
        in_specs=[x_spec] + grp + grp
        + [pl.BlockSpec((None, GROUP_W, D_MODEL), lambda i: (li, 0, 0))] + _mlp_specs(layer),
        out_specs=o_spec,
        out_shape=jax.ShapeDtypeStruct((b * s, D_MODEL), F32),
        scratch_shapes=_merge_scratch(tm) + _skew_scratch(tm),
        compiler_params=_params("arbitrary"),
        name="merge_mlp",
    )(x.reshape(b * s, D_MODEL), *os_, *ls_, wo, gmlp, wu, wd)
    return y.reshape(b, s, D_MODEL)


def _merge(x, os_, ls_, wo, li, *, tm, dils):
    b, s, _ = x.shape
    grp = [pl.BlockSpec((1, d, tm // d, GROUP_W), lambda i, j: (i, 0, j, 0)) for d in dils]
    return pl.pallas_call(
        functools.partial(_merge_kernel, tm=tm, dils=dils),
        grid=(b, s // tm),
        in_specs=[pl.BlockSpec((1, tm, D_MODEL), lambda i, j: (i, j, 0))] + grp + grp
        + [pl.BlockSpec((None, GROUP_W, D_MODEL), lambda i, j: (li, 0, 0))],
        out_specs=pl.BlockSpec((1, tm, D_MODEL), lambda i, j: (i, j, 0)),
        out_shape=jax.ShapeDtypeStruct((b, s, D_MODEL), F32),
        scratch_shapes=_merge_scratch(tm),
        compiler_params=_params("arbitrary", "arbitrary"),
        name="merge_out_proj",
    )(x, *os_, *ls_, wo)


def _cache_kernel(c_ref, q_ref, kn_ref, vn_ref, kvt_ref, *rest, n, dil, spb, aliased):
    co_ref, o_ref, lse_ref = rest[1:] if aliased else rest
    b0 = pl.program_id(0) * spb
    hw = GROUP_W // 2
    nh = hw // HEAD_DIM
    sub = lax.broadcasted_iota(jnp.int32, (nh, hw), 0)
    ln = lax.broadcasted_iota(jnp.int32, (nh, hw), 1)
    own = (ln // HEAD_DIM) == sub
    e = lax.broadcasted_iota(jnp.int32, (nh, n), 1)
    key_used = (n - e) % dil == 0
    bl = lax.broadcasted_iota(jnp.int32, kvt_ref.shape[2:], 1)
    en = lax.broadcasted_iota(jnp.int32, (hw, n), 1)

    def one_sample(bb, carry):
        kt = c_ref[0, bb, 0]
        vt = c_ref[0, bb, 1]
        qb = jnp.where(own, jnp.broadcast_to(q_ref[bb], (nh, hw)), 0.0)
        s = jnp.dot(qb.astype(BF16), kt.astype(BF16), preferred_element_type=F32)
        s = jnp.where(key_used, s, NEG)
        s_new = jnp.sum(qb * kn_ref[bb], axis=1, keepdims=True)
        m = jnp.maximum(jnp.max(s, axis=1, keepdims=True), s_new)
        pr = jnp.exp2(s - m)
        p_new = jnp.exp2(s_new - m)
        l = jnp.sum(pr, axis=1, keepdims=True) + p_new
        o = lax.dot_general(pr.astype(BF16), vt.astype(BF16), (((1,), (1,)), ((), ())),
                            preferred_element_type=F32)
        o = (o + p_new * vn_ref[bb]) / l
        o_ref[bb] = jnp.sum(jnp.where(own, o, 0.0), axis=0, keepdims=True)
        lse_ref[bb] = jnp.sum(jnp.where(own, m + jnp.log2(l), 0.0), axis=0, keepdims=True)
        for part, old in enumerate((kt, vt)):
            new_col = jnp.sum(jnp.where(bl == b0 + bb, kvt_ref[0, part], 0.0), axis=1, keepdims=True)
            co_ref[0, bb, part] = jnp.where(en == n - 1, new_col, pltpu.roll(old, n - 1, 1))
        return carry

    lax.fori_loop(0, spb, one_sample, 0)


def _cache_step(cache, layer, prev_out, q, kn, vn, kvt, gi, n, dil):
    nl, nb = cache.shape[:2]
    hw = GROUP_W // 2
    spb = max(1, min(nb, CACHE_BLOCK_BYTES // (2 * hw * n * 4)))
    aliased = prev_out is not None
    cblk = pl.BlockSpec((1, spb, 2, hw, n), lambda i, j: (layer, i, 0, j, 0))
    row = pl.BlockSpec((spb, 1, hw), lambda i, j: (i, 0, 2 * gi + j))
    in_specs = [cblk, row, row, row, pl.BlockSpec((1, 2, hw, nb), lambda i, j: (gi, 0, j, 0))]
    args = [cache, q, kn, vn, kvt]
    if aliased:
        in_specs.append(pl.BlockSpec(memory_space=pl.ANY))
        args.append(prev_out)
    orow = pl.BlockSpec((spb, 1, hw), lambda i, j: (i, 0, j))
    return pl.pallas_call(
        functools.partial(_cache_kernel, n=n, dil=dil, spb=spb, aliased=aliased),
        grid=(nb // spb, 2),
        in_specs=in_specs,
        out_specs=[cblk, orow, orow],
        out_shape=[
            jax.ShapeDtypeStruct(cache.shape, F32),
            jax.ShapeDtypeStruct((nb, 1, GROUP_W), F32),
            jax.ShapeDtypeStruct((nb, 1, GROUP_W), F32),
        ],
        input_output_aliases={5: 0} if aliased else {},
        compiler_params=_params("arbitrary", "arbitrary"),
        name="cache_step",
    )(*args)


def kernel(x_prompt, x_sample, state_pool, cache_kv_w128, cache_kv_w512, cache_kv_w2048, norm_mix,
           norm_mlp, pool_w, pool_scale, attn_w_qkv, attn_q_norm, attn_k_norm, attn_w_o, mlp_w_up,
           mlp_w_down):
    bsz, seq, _ = x_prompt.shape
    nb = x_sample.shape[0]
    past = PAST_LEN
    dils = tuple(d for _, d in ATTN_PATTERNS)
    ones_sample = (1, 1, 1)

    wu = mlp_w_up.astype(BF16)
    wd = mlp_w_down.astype(BF16)
    wqkv = attn_w_qkv.astype(BF16)
    wo = attn_w_o.astype(BF16)
    wp = pool_w.astype(BF16)
    head = lax.broadcasted_iota(jnp.int32, (MXU_DIM, MXU_DIM), 0) // HEAD_DIM
    ones = (head == head.T).astype(BF16)
    cos_p, sin_p = _rope_tables(jnp.arange(seq))
    cos_s, sin_s = _rope_tables(jnp.full((nb,), past))

    caches = [jnp.transpose(c, (0, 1, 3, 4, 5, 2)).reshape(c.shape[0], nb, 2, GROUP_W, c.shape[2])
              for c in (cache_kv_w128, cache_kv_w512, cache_kv_w2048)]

    xp = x_prompt
    xs = x_sample.reshape(nb, D_MODEL)
    pool_p, pool_s, kv_p = [], [], [[] for _ in range(N_GROUPS)]
    new_caches = [None] * N_GROUPS
    for layer in range(DEPTH):
        li = layer // 2
        g_mix = norm_mix[layer][None, :]
        g_mlp = norm_mlp[layer][None, :]
        if layer % 2 == 0:
            sc = pool_scale[li][None, :]
            xp, st = _pool_mlp(xp, g_mix, wp[li], sc, g_mlp, wu, wd, layer, ROW_TILE)
            pool_p.append(st)
            st_in = jnp.transpose(state_pool[li], (1, 0, 2))
            xs, st = _pool_sample(xs, st_in, g_mix, wp[li], sc, past)
            pool_s.append(jnp.transpose(st, (1, 0, 2)))
        else:
            gq = _gain_tables(attn_q_norm[li])
            gk = _gain_tables(attn_k_norm[li])
            outs = _qkv(xp, g_mix, wqkv, li, gq, gk, cos_p, sin_p, ones, tm=ROW_TILE, dils=dils,
                        out_dtype=BF16, tail_rows=TAIL)
            os_, ls_ = [], []
            for gi, (w, d) in enumerate(ATTN_PATTERNS):
                q, k, v = (outs[part * N_GROUPS + gi].reshape(bsz * d, seq // d, GROUP_W)
                           for part in range(3))
                o, lse = _attn(q, k, v, min(ATTN_TQ, seq // d))
                os_.append(o.reshape(bsz, d, seq // d, GROUP_W))
                ls_.append(lse.reshape(bsz, d, seq // d, GROUP_W))
                n_out = min(w, seq)
                kv_p[gi].append(outs[9][:, TAIL - n_out:, gi * 2 * GROUP_W:(gi + 1) * 2 * GROUP_W]
                                .reshape(bsz, n_out, 2, HEADS, HEAD_DIM))
            xp = _merge_mlp(xp, os_, ls_, wo, li, g_mlp, wu, wd, layer, tm=ROW_TILE, dils=dils)
            outs = _qkv(xs[None], g_mix, wqkv, li, gq, gk, cos_s, sin_s, ones, tm=nb, dils=ones_sample,
                        out_dtype=F32, tail_rows=0)
            q, kn, vn = (jnp.concatenate([outs[part * N_GROUPS + gi][0, 0] for gi in range(N_GROUPS)],
                                         axis=1)[:, None, :] for part in range(3))
            kvt = jnp.stack([jnp.stack([outs[part * N_GROUPS + gi][0, 0].T for part in (1, 2)])
                             for gi in range(N_GROUPS)])
            os_, ls_ = [], []
            for gi, (w, d) in enumerate(ATTN_PATTERNS):
                n = caches[gi].shape[-1]
                new_caches[gi], o, lse = _cache_step(caches[gi], li, new_caches[gi], q, kn, vn, kvt,
                                                     gi, n, d)
                os_.append(o.reshape(1, 1, nb, GROUP_W))
                ls_.append(lse.reshape(1, 1, nb, GROUP_W))
            xs = _merge(xs[None], os_, ls_, wo, li, tm=nb, dils=ones_sample)[0]
        xs = _mlp(xs, g_mlp, wu, wd, layer, nb)

    kv_s = [jnp.transpose(c.reshape(c.shape[0], nb, 2, HEADS, HEAD_DIM, c.shape[-1]), (0, 1, 5, 2, 3, 4))
            for c in new_caches]
    kv_p = [jnp.stack(t) for t in kv_p]
    return (xp, xs.reshape(nb, 1, D_MODEL), jnp.stack(pool_p), jnp.stack(pool_s),
            kv_p[0], kv_s[0], kv_p[1], kv_s[1], kv_p[2], kv_s[2])
```

```python
import functools

import jax
import jax.numpy as jnp
import numpy as np
from jax import lax
from jax.experimental import pallas as pl
from jax.experimental.pallas import tpu as pltpu

F32 = jnp.float32
BF16 = jnp.bfloat16

D_MODEL = 1024
D_FF = 4 * D_MODEL
DEPTH = 4
POOL_WINDOWS = (2, 4, 8, 16)
POOL_GC = D_MODEL // len(POOL_WINDOWS)
POOL_STATE = max(POOL_WINDOWS) - 1
ATTN_PATTERNS = ((128, 1), (512, 4), (2048, 16))
N_GROUPS = len(ATTN_PATTERNS)
HEAD_DIM = 64
HEADS = 8
GROUP_W = HEADS * HEAD_DIM
QKV_W = 3 * N_GROUPS * GROUP_W
N_KEYS_M1 = 128
ROPE_THETA = 10000.0
PAST_LEN = 8192
EPS = 1e-6
NEG = -1e30
LOG2E = 1.4426950408889634
Q_SCALE = HEAD_DIM ** -0.5 * LOG2E

LANES = 128
MXU_DIM = 256
VMEM_LIMIT = 56 * 1024 * 1024
ROW_TILE = 512
POOL_TILE = 256
FF_CHUNK = 512
DOWN_CHUNK = 256
ATTN_TQ = 512
Q_SUB = 128
TAIL = max(w for w, _ in ATTN_PATTERNS)


def _params(*sem):
    return pltpu.CompilerParams(dimension_semantics=sem, vmem_limit_bytes=VMEM_LIMIT)


def _rmsnorm(x, g):
    return x * lax.rsqrt(jnp.mean(x * x, axis=-1, keepdims=True) + EPS) * g


def _mlp_residual(x, g, wu_ref, wd_ref, a_ref, o_ref, side=()):
    h = _rmsnorm(x, g).astype(BF16)
    n_up = D_FF // FF_CHUNK
    n_down = D_MODEL // DOWN_CHUNK
    slots = n_up + n_down - 1

    def run_side(slot):
        for j, piece in enumerate(side):
            if j * slots // len(side) == slot:
                piece()

    for c in range(n_up):
        cols = slice(c * FF_CHUNK, (c + 1) * FF_CHUNK)
        u = jnp.dot(h, wu_ref[:, cols], preferred_element_type=F32)
        a_ref[:, cols] = jnp.square(jnp.maximum(u, 0.0)).astype(BF16)
        run_side(c)
    for c in range(n_down):
        cols = slice(c * DOWN_CHUNK, (c + 1) * DOWN_CHUNK)
        o_ref[:, cols] = x[:, cols] + jnp.dot(a_ref[...], wd_ref[:, cols], preferred_element_type=F32)
        if c < n_down - 1:
            run_side(n_up + c)


def _mlp_kernel(x_ref, g_ref, wu_ref, wd_ref, o_ref, a_ref):
    _mlp_residual(x_ref[...], g_ref[...], wu_ref, wd_ref, a_ref, o_ref)


def _mlp(x, g, wu, wd, layer, tm):
    n = x.shape[0]
    return pl.pallas_call(
        _mlp_kernel,
        grid=(n // tm,),
        in_specs=[
            pl.BlockSpec((tm, D_MODEL), lambda i: (i, 0)),
            pl.BlockSpec((1, D_MODEL), lambda i: (0, 0)),
            pl.BlockSpec((None, D_MODEL, D_FF), lambda i: (layer, 0, 0)),
            pl.BlockSpec((None, D_FF, D_MODEL), lambda i: (layer, 0, 0)),
        ],
        out_specs=pl.BlockSpec((tm, D_MODEL), lambda i: (i, 0)),
        out_shape=jax.ShapeDtypeStruct((n, D_MODEL), F32),
        scratch_shapes=[pltpu.VMEM((tm, D_FF), BF16)],
        compiler_params=_params("arbitrary"),
        name="mlp",
    )(x, g, wu, wd)


def _skew_specs(n_tiles, tm):
    x_spec = pl.BlockSpec((tm, D_MODEL), lambda i: (jnp.minimum(i, n_tiles - 1), 0))
    o_spec = pl.BlockSpec((tm, D_MODEL), lambda i: (jnp.maximum(i - 1, 0), 0))
    return x_spec, o_spec


def _skew_scratch(tm):
    return [pltpu.VMEM((tm, D_MODEL), F32), pltpu.VMEM((tm, D_MODEL), F32), pltpu.VMEM((tm, D_FF), BF16)]


def _mlp_specs(layer):
    return [pl.BlockSpec((1, D_MODEL), lambda i: (0, 0)),
            pl.BlockSpec((None, D_MODEL, D_FF), lambda i: (layer, 0, 0)),
            pl.BlockSpec((None, D_FF, D_MODEL), lambda i: (layer, 0, 0))]


def _pool_mlp_kernel(x_ref, gmix_ref, w_ref, sc_ref, gmlp_ref, wu_ref, wd_ref, *rest,
                     tm, tiles_per_seq, n_tiles, buf_lens, aliased):
    ng = len(buf_lens)
    c_refs, (qkn_ref, kvt_ref) = rest[:ng], rest[ng:ng + 2]
    outs = rest[ng + 2 + (ng if aliased else 0):]
    o_ref, st_ref = outs[:2]
    co_refs, ol_ref = outs[2:2 + ng], outs[2 + ng]
    ext_ref, mid_next, mid_cur, a_ref = outs[3 + ng:]
    i = pl.program_id(0)
    halo = POOL_STATE + 1

    @pl.when(i == 0)
    def _():
        ext_ref[0:halo, :] = jnp.zeros((halo, D_MODEL), F32)
        mid_cur[...] = jnp.zeros((tm, D_MODEL), F32)

    s = jnp.minimum(i, n_tiles - 1) % tiles_per_seq
    ext_ref[0:halo, :] = jnp.where(s == 0, 0.0, ext_ref[0:halo, :])

    def normalise():
        ext_ref[halo:halo + tm, :] = _rmsnorm(x_ref[...], gmix_ref[...])

    def pool_block(gi, r0, nr):
        w = POOL_WINDOWS[gi]
        cols = slice(gi * POOL_GC, (gi + 1) * POOL_GC)
        pos = s * tm + r0 + lax.broadcasted_iota(jnp.int32, (nr, 1), 0)
        cur = ext_ref[halo + r0:halo + r0 + nr, cols]
        acc = cur
        for k in range(1, w):
            acc = acc + ext_ref[halo + r0 - k:halo + r0 - k + nr, cols]
        cnt = jnp.minimum(pos + 1, w).astype(F32)
        d = (acc / cnt - cur).astype(BF16)
        y = jnp.dot(d, w_ref[gi], preferred_element_type=F32)
        mid_next[r0:r0 + nr, cols] = x_ref[r0:r0 + nr, cols] + y * sc_ref[:, cols]

    t = jnp.minimum(i, n_tiles - 1)
    sample, half = t // 2, t % 2

    def buffer_group(g):
        kt, vt = c_refs[g][0, 0, 0], c_refs[g][0, 0, 1]
        lane = lax.broadcasted_iota(jnp.int32, kvt_ref.shape[3:], 1)
        new_k = jnp.sum(jnp.where(lane == sample, kvt_ref[half, g, 0], 0.0), axis=1, keepdims=True)
        new_v = jnp.sum(jnp.where(lane == sample, kvt_ref[half, g, 1], 0.0), axis=1, keepdims=True)
        rk, rv, o_row, lse_row = _sample_attend_and_roll(
            kt, vt, qkn_ref[0, 0, g:g + 1, :], qkn_ref[0, 0, ng + g:ng + g + 1, :],
            qkn_ref[0, 0, 2 * ng + g:2 * ng + g + 1, :], new_k, new_v, buf_lens[g], ATTN_PATTERNS[g][1])
        co_refs[g][0, 0, 0] = rk
        co_refs[g][0, 0, 1] = rv
        ol_ref[0, 0, g:g + 1, :] = o_row
        ol_ref[0, 0, ng + g:ng + g + 1, :] = lse_row

    side = [normalise]
    for gi, splits in ((3, 4), (2, 2), (1, 1), (0, 1)):
        nr = tm // splits
        side += [functools.partial(pool_block, gi, p * nr, nr) for p in range(splits)]
    for g in range(ng):
        side.insert(1 + 3 * g, functools.partial(buffer_group, ng - 1 - g))
    _mlp_residual(mid_cur[...], gmlp_ref[...], wu_ref, wd_ref, a_ref, o_ref, side)

    st_ref[0] = ext_ref[tm + 1:tm + halo, :]
    ext_ref[0:halo, :] = ext_ref[tm:tm + halo, :]
    mid_cur[...] = mid_next[...]


def _pool_mlp(x, gmix, w, sc, gmlp, wu, wd, layer, tm, caches, li, prev_rolled, qkn, kvt):
    b, s, _ = x.shape
    tps = s // tm
    n_tiles = b * tps
    nb = qkn.shape[0]
    hw = GROUP_W // 2
    assert n_tiles == 2 * nb, "one half of one sample's heads rides along with every row tile"
    ng = len(caches)
    buf_lens = tuple(c.shape[-1] for c in caches)
    aliased = prev_rolled is not None
    x_spec, o_spec = _skew_specs(n_tiles, tm)
    const = lambda i: (0, 0)

    def half_sample(i):
        t = jnp.minimum(i, n_tiles - 1)
        return t // 2, t % 2

    def cache_map(i):
        smp, half = half_sample(i)
        return (li, smp, 0, half, 0)

    def row_map(i):
        smp, half = half_sample(i)
        return (smp, half, 0, 0)

    cblk = [pl.BlockSpec((1, 1, 2, hw, n), cache_map) for n in buf_lens]
    in_specs = [
        x_spec,
        pl.BlockSpec((1, D_MODEL), const),
        pl.BlockSpec((len(POOL_WINDOWS), POOL_GC, POOL_GC), lambda i: (0, 0, 0)),
        pl.BlockSpec((1, D_MODEL), const),
    ] + _mlp_specs(layer) + cblk + [
        pl.BlockSpec((1, 1, 3 * ng, hw), row_map),
        pl.BlockSpec(kvt.shape, lambda i: (0, 0, 0, 0, 0)),
    ]
    args = [x.reshape(b * s, D_MODEL), gmix, w, sc, gmlp, wu, wd, *caches, qkn, kvt]
    aliases = {}
    if aliased:
        aliases = {len(args) + g: 2 + g for g in range(ng)}
        in_specs += [pl.BlockSpec(memory_space=pl.ANY)] * ng
        args += list(prev_rolled)
    res = pl.pallas_call(
        functools.partial(_pool_mlp_kernel, tm=tm, tiles_per_seq=tps, n_tiles=n_tiles,
                          buf_lens=buf_lens, aliased=aliased),
        grid=(n_tiles + 1,),
        in_specs=in_specs,
        out_specs=[
            o_spec,
            pl.BlockSpec((1, POOL_STATE, D_MODEL),
                         lambda i: (jnp.minimum(i, n_tiles - 1) // tps, 0, 0)),
        ] + cblk + [pl.BlockSpec((1, 1, 2 * ng, hw), row_map)],
        out_shape=[
            jax.ShapeDtypeStruct((b * s, D_MODEL), F32),
            jax.ShapeDtypeStruct((b, POOL_STATE, D_MODEL), F32),
        ] + [jax.ShapeDtypeStruct(c.shape, F32) for c in caches]
        + [jax.ShapeDtypeStruct((nb, 2, 2 * ng, hw), F32)],
        input_output_aliases=aliases,
        scratch_shapes=[pltpu.VMEM((tm + POOL_STATE + 1, D_MODEL), F32)] + _skew_scratch(tm),
        compiler_params=_params("arbitrary"),
        name="pool_mlp",
    )(*args)
    return res[0].reshape(b, s, D_MODEL), res[1], list(res[2:2 + ng]), res[2 + ng]


def _pool_sample_kernel(x_ref, st_ref, g_ref, w_ref, sc_ref, o_ref, nst_ref, *, pos):
    x = x_ref[...]
    xn = _rmsnorm(x, g_ref[...])
    for gi, w in enumerate(POOL_WINDOWS):
        cols = slice(gi * POOL_GC, (gi + 1) * POOL_GC)
        cur = xn[:, cols]
        acc = cur
        for k in range(1, w):
            acc = acc + st_ref[POOL_STATE - k, :, cols]
        d = acc / float(min(pos + 1, w)) - cur
        y = jnp.dot(d.astype(BF16), w_ref[gi], preferred_element_type=F32)
        o_ref[:, cols] = x[:, cols] + y * sc_ref[:, cols]
    for k in range(POOL_STATE - 1):
        nst_ref[k] = st_ref[k + 1]
    nst_ref[POOL_STATE - 1] = xn


def _pool_sample(x, st, g, w, sc, pos):
    n = x.shape[0]
    return pl.pallas_call(
        functools.partial(_pool_sample_kernel, pos=pos),
        out_shape=[
            jax.ShapeDtypeStruct((n, D_MODEL), F32),
            jax.ShapeDtypeStruct((POOL_STATE, n, D_MODEL), F32),
        ],
        compiler_params=pltpu.CompilerParams(vmem_limit_bytes=VMEM_LIMIT),
        name="pool_sample",
    )(x, st, g, w, sc)


def _rope_tables(pos):
    half = HEAD_DIM // 2
    inv = ROPE_THETA ** (-np.arange(half, dtype=np.float64) * 2.0 / HEAD_DIM)
    ang = pos.astype(np.float64)[:, None] * inv[None, :]
    cos, sin = np.cos(ang), np.sin(ang)
    return (jnp.asarray(np.concatenate([cos, cos, cos, cos], axis=1), F32),
            jnp.asarray(np.concatenate([-sin, sin, -sin, sin], axis=1), F32))


def _gain_tables(gain):
    half = HEAD_DIM // 2
    swapped = jnp.concatenate([gain[:, half:], gain[:, :half]], axis=1)
    return jnp.concatenate([gain, gain], axis=1), jnp.concatenate([swapped, swapped], axis=1)


def _qkv_kernel(x_ref, g_ref, w_ref, gq_ref, gqs_ref, gk_ref, gks_ref, cos_ref, sin_ref, ones_ref,
                *rest, tm, dils, out_dtype, with_tail):
    outs = rest[:9]
    tail_ref = rest[9] if with_tail else None
    stage_ref = rest[-1]
    x = x_ref[0]
    h = _rmsnorm(x, g_ref[...]).astype(BF16)
    cosf = cos_ref[...]
    sinf = sin_ref[...]
    lane = lax.broadcasted_iota(jnp.int32, (tm, LANES), 1)
    first_half = (lane % HEAD_DIM) < (HEAD_DIM // 2)
    for part in range(3):
        for gi in range(N_GROUPS):
            c0 = (part * N_GROUPS + gi) * GROUP_W
            res = jnp.dot(h, w_ref[:, c0:c0 + GROUP_W], preferred_element_type=F32)
            if part < 2:
                gain = (gq_ref, gk_ref)[part][gi:gi + 1, :]
                gain_sw = (gqs_ref, gks_ref)[part][gi:gi + 1, :]
                ta = gain * cosf
                tb = gain_sw * sinf
                if part == 0:
                    ta = ta * Q_SCALE
                    tb = tb * Q_SCALE
                sq = (res * res).astype(BF16)
                pieces = []
                for hh in range(GROUP_W // MXU_DIM):
                    ss = jnp.dot(sq[:, hh * MXU_DIM:(hh + 1) * MXU_DIM], ones_ref[...],
                                 preferred_element_type=F32)
                    r = lax.rsqrt(ss * (1.0 / HEAD_DIM) + EPS)
                    for sl in range(MXU_DIM // LANES):
                        l0 = hh * MXU_DIM + sl * LANES
                        xs = res[:, l0:l0 + LANES]
                        swapped = jnp.where(first_half,
                                            pltpu.roll(xs, LANES - HEAD_DIM // 2, 1),
                                            pltpu.roll(xs, HEAD_DIM // 2, 1))
                        pieces.append(r[:, sl * LANES:(sl + 1) * LANES] * (xs * ta + swapped * tb))
                res = jnp.concatenate(pieces, axis=1)
            if with_tail and part > 0:
                t0 = (gi * 2 + part - 1) * GROUP_W
                tail_ref[0, :, t0:t0 + GROUP_W] = res
            out_ref = outs[part * N_GROUPS + gi]
            d = dils[gi]
            if d == 1:
                out_ref[0, 0] = res.astype(out_dtype)
            else:
                for c in range(GROUP_W // LANES):
                    stage_ref[c] = res[:, c * LANES:(c + 1) * LANES]
                for r_ in range(d):
                    for c in range(GROUP_W // LANES):
                        out_ref[0, r_, :, c * LANES:(c + 1) * LANES] = (
                            stage_ref[c, pl.ds(r_, tm // d, stride=d), :].astype(out_dtype))


def _qkv(x, g, w, li, gq, gk, cosf, sinf, ones, *, tm, dils, out_dtype, tail_rows):
    b, s, _ = x.shape
    nt = s // tm
    with_tail = tail_rows > 0
    gq_t, gq_s = gq
    gk_t, gk_s = gk
    const2 = lambda i, j: (0, 0)
    in_specs = [
        pl.BlockSpec((1, tm, D_MODEL), lambda i, j: (i, j, 0)),
        pl.BlockSpec((1, D_MODEL), const2),
        pl.BlockSpec((None, D_MODEL, QKV_W), lambda i, j: (li, 0, 0)),
        pl.BlockSpec((N_GROUPS, LANES), const2),
        pl.BlockSpec((N_GROUPS, LANES), const2),
        pl.BlockSpec((N_GROUPS, LANES), const2),
        pl.BlockSpec((N_GROUPS, LANES), const2),
        pl.BlockSpec((tm, LANES), lambda i, j: (j, 0)),
        pl.BlockSpec((tm, LANES), lambda i, j: (j, 0)),
        pl.BlockSpec((MXU_DIM, MXU_DIM), const2),
    ]
    out_specs, out_shape = [], []
    for _ in range(3):
        for d in dils:
            out_specs.append(pl.BlockSpec((1, d, tm // d, GROUP_W), lambda i, j: (i, 0, j, 0)))
            out_shape.append(jax.ShapeDtypeStruct((b, d, s // d, GROUP_W), out_dtype))
    if with_tail:
        t0 = nt - tail_rows // tm
        out_specs.append(pl.BlockSpec((1, tm, 2 * N_GROUPS * GROUP_W),
                                      lambda i, j: (i, jnp.maximum(j - t0, 0), 0)))
        out_shape.append(jax.ShapeDtypeStruct((b, tail_rows, 2 * N_GROUPS * GROUP_W), F32))
    return pl.pallas_call(
        functools.partial(_qkv_kernel, tm=tm, dils=dils, out_dtype=out_dtype, with_tail=with_tail),
        grid=(b, nt),
        in_specs=in_specs,
        out_specs=out_specs,
        out_shape=out_shape,
        scratch_shapes=[pltpu.VMEM((GROUP_W // LANES, tm, LANES), F32)],
        compiler_params=_params("arbitrary", "arbitrary"),
        name="qkv",
    )(x, g, w, gq_t, gq_s, gk_t, gk_s, cosf, sinf, ones)


def _attn_kernel(q_ref, kp_ref, kc_ref, vp_ref, vc_ref, o_ref, lse_ref, kcat_ref, vcat_ref, *, tq):
    j = pl.program_id(1)
    kcat_ref[0:Q_SUB, :] = kp_ref[0]
    kcat_ref[Q_SUB:Q_SUB + tq, :] = kc_ref[0]
    vcat_ref[0:Q_SUB, :] = vp_ref[0]
    vcat_ref[Q_SUB:Q_SUB + tq, :] = vc_ref[0]

    nk = 2 * Q_SUB
    nq = 2 * Q_SUB
    row = lax.broadcasted_iota(jnp.int32, (nq, nk), 0) & (Q_SUB - 1)
    col = lax.broadcasted_iota(jnp.int32, (nq, nk), 1)
    dist = row + Q_SUB - col
    band = (dist >= 0) & (dist <= N_KEYS_M1)
    lane = lax.broadcasted_iota(jnp.int32, (Q_SUB, LANES), 1)
    low_head = lane < HEAD_DIM
    zero_q = jnp.zeros((Q_SUB, LANES), BF16)
    ones_v = jnp.ones((nk, LANES), BF16)

    def sub_block(sb, carry):
        r0 = pl.multiple_of(sb * Q_SUB, Q_SUB)
        valid = band & ((col >= Q_SUB) | (sb + j > 0))
        for p in range(GROUP_W // LANES):
            lanes = slice(p * LANES, (p + 1) * LANES)
            qp = q_ref[0, pl.ds(r0, Q_SUB), lanes]
            kp = kcat_ref[pl.ds(r0, nk), lanes]
            vaug = jnp.concatenate([vcat_ref[pl.ds(r0, nk), lanes], ones_v], axis=1)
            q2 = jnp.concatenate([jnp.where(low_head, qp, zero_q), jnp.where(low_head, zero_q, qp)], axis=0)
            s = lax.dot_general(q2, kp, (((1,), (1,)), ((), ())), preferred_element_type=F32)
            s = jnp.where(valid, s, NEG)
            m = jnp.max(s, axis=1, keepdims=True)
            pr = jnp.exp2(s - m).astype(BF16)
            ov = jnp.dot(pr, vaug, preferred_element_type=F32)
            den = ov[:, LANES:]
            on = ov[:, :LANES] / den
            lse = m + jnp.log2(den)
            o_ref[0, pl.ds(r0, Q_SUB), lanes] = jnp.where(low_head, on[:Q_SUB], on[Q_SUB:])
            lse_ref[0, pl.ds(r0, Q_SUB), lanes] = jnp.where(low_head, lse[:Q_SUB], lse[Q_SUB:])
        return carry

    lax.fori_loop(0, tq // Q_SUB, sub_block, 0)


def _attn(q, k, v, tq):
    nseq, length, _ = q.shape
    ratio = tq // Q_SUB
    cur = lambda i, j: (i, j, 0)
    prev = lambda i, j: (i, jnp.maximum(j * ratio - 1, 0), 0)
    return pl.pallas_call(
        functools.partial(_attn_kernel, tq=tq),
        grid=(nseq, length // tq),
        in_specs=[
            pl.BlockSpec((1, tq, GROUP_W), cur),
            pl.BlockSpec((1, Q_SUB, GROUP_W), prev),
            pl.BlockSpec((1, tq, GROUP_W), cur),
            pl.BlockSpec((1, Q_SUB, GROUP_W), prev),
            pl.BlockSpec((1, tq, GROUP_W), cur),
        ],
        out_specs=[pl.BlockSpec((1, tq, GROUP_W), cur), pl.BlockSpec((1, tq, GROUP_W), cur)],
        out_shape=[jax.ShapeDtypeStruct((nseq, length, GROUP_W), F32)] * 2,
        scratch_shapes=[pltpu.VMEM((tq + Q_SUB, GROUP_W), BF16)] * 2,
        compiler_params=_params("arbitrary", "arbitrary"),
        name="attn_prompt",
    )(q, k, k, v, v)


def _merge_pieces(load_x, store_y, o_refs, l_refs, wo_ref, und, tm, dils):
    nc = GROUP_W // LANES

    def rows(ref, d, scr, r0, nr):
        if d == 1:
            return ref[0, 0, r0:r0 + nr, :]
        for r_ in range(d):
            for c in range(nc):
                scr[c, pl.ds(r0 + r_, nr // d, stride=d), :] = (
                    ref[0, r_, r0 // d:(r0 + nr) // d, c * LANES:(c + 1) * LANES])
        return jnp.concatenate([scr[c, r0:r0 + nr, :] for c in range(nc)], axis=1)

    def merge_rows(r0, nr):
        os_ = [rows(o_refs[g], dils[g], und[2 * g], r0, nr) for g in range(N_GROUPS)]
        ls_ = [rows(l_refs[g], dils[g], und[2 * g + 1], r0, nr) for g in range(N_GROUPS)]
        mx = jnp.maximum(jnp.maximum(ls_[0], ls_[1]), ls_[2])
        ws = [jnp.exp2(l - mx) for l in ls_]
        num = ws[0] * os_[0] + ws[1] * os_[1] + ws[2] * os_[2]
        den = ws[0] + ws[1] + ws[2]
        merged = (num / den).astype(BF16)
        store_y(r0, nr, load_x(r0, nr) + jnp.dot(merged, wo_ref[...], preferred_element_type=F32))

    nr = min(tm, Q_SUB)
    return [functools.partial(merge_rows, r0, nr) for r0 in range(0, tm, nr)]


def _merge_scratch(tm):
    return [pltpu.VMEM((GROUP_W // LANES, tm, LANES), F32)] * (2 * N_GROUPS)


def _merge_kernel(x_ref, o0, o1, o2, l0, l1, l2, wo_ref, y_ref, *scratch, tm, dils):
    def store_y(r0, nr, v):
        y_ref[0, r0:r0 + nr, :] = v

    for piece in _merge_pieces(lambda r0, nr: x_ref[0, r0:r0 + nr, :], store_y, (o0, o1, o2),
                               (l0, l1, l2), wo_ref, scratch, tm, dils):
        piece()


def _merge_mlp_kernel(x_ref, o0, o1, o2, l0, l1, l2, wo_ref, gmlp_ref, wu_ref, wd_ref, y_ref,
                      *scratch, tm, dils):
    und = scratch[:2 * N_GROUPS]
    mid_next, mid_cur, a_ref = scratch[2 * N_GROUPS:]

    @pl.when(pl.program_id(0) == 0)
    def _():
        mid_cur[...] = jnp.zeros((tm, D_MODEL), F32)

    def store_mid(r0, nr, v):
        mid_next[r0:r0 + nr, :] = v

    side = _merge_pieces(lambda r0, nr: x_ref[r0:r0 + nr, :], store_mid, (o0, o1, o2), (l0, l1, l2),
                         wo_ref, und, tm, dils)
    _mlp_residual(mid_cur[...], gmlp_ref[...], wu_ref, wd_ref, a_ref, y_ref, side)
    mid_cur[...] = mid_next[...]


def _merge_mlp(x, os_, ls_, wo, li, gmlp, wu, wd, layer, *, tm, dils):
    b, s, _ = x.shape
    tps = s // tm
    n_tiles = b * tps
    x_spec, o_spec = _skew_specs(n_tiles, tm)

    def grp_map(i):
        t = jnp.minimum(i, n_tiles - 1)
        return (t // tps, 0, t % tps, 0)

    grp = [pl.BlockSpec((1, d, tm // d, GROUP_W), grp_map) for d in dils]
    y = pl.pallas_call(
        functools.partial(_merge_mlp_kernel, tm=tm, dils=dils),
        grid=(n_tiles + 1,),
        in_specs=[x_spec] + grp + grp
        + [pl.BlockSpec((None, GROUP_W, D_MODEL), lambda i: (li, 0, 0))] + _mlp_specs(layer),
        out_specs=o_spec,
        out_shape=jax.ShapeDtypeStruct((b * s, D_MODEL), F32),
        scratch_shapes=_merge_scratch(tm) + _skew_scratch(tm),
        compiler_params=_params("arbitrary"),
        name="merge_mlp",
    )(x.reshape(b * s, D_MODEL), *os_, *ls_, wo, gmlp, wu, wd)
    return y.reshape(b, s, D_MODEL)


def _merge(x, os_, ls_, wo, li, *, tm, dils):
    b, s, _ = x.shape
    grp = [pl.BlockSpec((1, d, tm // d, GROUP_W), lambda i, j: (i, 0, j, 0)) for d in dils]
    return pl.pallas_call(
        functools.partial(_merge_kernel, tm=tm, dils=dils),
        grid=(b, s // tm),
        in_specs=[pl.BlockSpec((1, tm, D_MODEL), lambda i, j: (i, j, 0))] + grp + grp
        + [pl.BlockSpec((None, GROUP_W, D_MODEL), lambda i, j: (li, 0, 0))],
        out_specs=pl.BlockSpec((1, tm, D_MODEL), lambda i, j: (i, j, 0)),
        out_shape=jax.ShapeDtypeStruct((b, s, D_MODEL), F32),
        scratch_shapes=_merge_scratch(tm),
        compiler_params=_params("arbitrary", "arbitrary"),
        name="merge_out_proj",
    )(x, *os_, *ls_, wo)


def _sample_attend_and_roll(kt, vt, q_row, kn_row, vn_row, new_k, new_v, n, dil):
    hw = kt.shape[0]
    nh = hw // HEAD_DIM
    sub = lax.broadcasted_iota(jnp.int32, (nh, hw), 0)
    ln = lax.broadcasted_iota(jnp.int32, (nh, hw), 1)
    own = (ln // HEAD_DIM) == sub
    e = lax.broadcasted_iota(jnp.int32, (nh, n), 1)
    qb = jnp.where(own, jnp.broadcast_to(q_row, (nh, hw)), 0.0)
    s = jnp.dot(qb.astype(BF16), kt.astype(BF16), preferred_element_type=F32)
    s = jnp.where((n - e) % dil == 0, s, NEG)
    s_new = jnp.sum(qb * kn_row, axis=1, keepdims=True)
    m = jnp.maximum(jnp.max(s, axis=1, keepdims=True), s_new)
    pr = jnp.exp2(s - m)
    p_new = jnp.exp2(s_new - m)
    l = jnp.sum(pr, axis=1, keepdims=True) + p_new
    o = lax.dot_general(pr.astype(BF16), vt.astype(BF16), (((1,), (1,)), ((), ())),
                        preferred_element_type=F32)
    o = (o + p_new * vn_row) / l
    o_row = jnp.sum(jnp.where(own, o, 0.0), axis=0, keepdims=True)
    lse_row = jnp.sum(jnp.where(own, m + jnp.log2(l), 0.0), axis=0, keepdims=True)
    en = lax.broadcasted_iota(jnp.int32, (hw, n), 1)
    rolled_k = jnp.where(en == n - 1, new_k, pltpu.roll(kt, n - 1, 1))
    rolled_v = jnp.where(en == n - 1, new_v, pltpu.roll(vt, n - 1, 1))
    return rolled_k, rolled_v, o_row, lse_row


def kernel(x_prompt, x_sample, state_pool, cache_kv_w128, cache_kv_w512, cache_kv_w2048, norm_mix,
           norm_mlp, pool_w, pool_scale, attn_w_qkv, attn_q_norm, attn_k_norm, attn_w_o, mlp_w_up,
           mlp_w_down):
    bsz, seq, _ = x_prompt.shape
    nb = x_sample.shape[0]
    past = PAST_LEN
    dils = tuple(d for _, d in ATTN_PATTERNS)
    ones_sample = (1, 1, 1)

    wu = mlp_w_up.astype(BF16)
    wd = mlp_w_down.astype(BF16)
    wqkv = attn_w_qkv.astype(BF16)
    wo = attn_w_o.astype(BF16)
    wp = pool_w.astype(BF16)
    head = lax.broadcasted_iota(jnp.int32, (MXU_DIM, MXU_DIM), 0) // HEAD_DIM
    ones = (head == head.T).astype(BF16)
    cos_p, sin_p = _rope_tables(np.arange(seq))
    cos_s, sin_s = _rope_tables(np.full((nb,), past))

    caches = [jnp.transpose(c, (0, 1, 3, 4, 5, 2)).reshape(c.shape[0], nb, 2, GROUP_W, c.shape[2])
              for c in (cache_kv_w128, cache_kv_w512, cache_kv_w2048)]

    xp = x_prompt
    xs = x_sample.reshape(nb, D_MODEL)
    pool_p, pool_s, kv_p = [], [], [[] for _ in range(N_GROUPS)]
    new_caches = None
    hw = GROUP_W // 2
    for layer in range(DEPTH):
        li = layer // 2
        g_mix = norm_mix[layer][None, :]
        g_mlp = norm_mlp[layer][None, :]
        gq = _gain_tables(attn_q_norm[li])
        gk = _gain_tables(attn_k_norm[li])
        if layer % 2 == 0:
            sc = pool_scale[li][None, :]
            st_in = jnp.transpose(state_pool[li], (1, 0, 2))
            xs, st = _pool_sample(xs, st_in, g_mix, wp[li], sc, past)
            pool_s.append(jnp.transpose(st, (1, 0, 2)))
            xs = _mlp(xs, g_mlp, wu, wd, layer, nb)
            outs = _qkv(xs[None], norm_mix[layer + 1][None, :], wqkv, li, gq, gk, cos_s, sin_s, ones,
                        tm=nb, dils=ones_sample, out_dtype=F32, tail_rows=0)
            rows = jnp.stack([o[0, 0] for o in outs[:9]], axis=1)
            qkn = jnp.transpose(rows.reshape(nb, 9, 2, hw), (0, 2, 1, 3))
            kvt = jnp.transpose(rows[:, N_GROUPS:].reshape(nb, 2, N_GROUPS, 2, hw), (3, 2, 1, 4, 0))
            xp, st, new_caches, sample_attn = _pool_mlp(xp, g_mix, wp[li], sc, g_mlp, wu, wd, layer,
                                                        POOL_TILE, caches, li, new_caches, qkn, kvt)
            pool_p.append(st)
        else:
            outs = _qkv(xp, g_mix, wqkv, li, gq, gk, cos_p, sin_p, ones, tm=ROW_TILE, dils=dils,
                        out_dtype=BF16, tail_rows=TAIL)
            os_, ls_ = [], []
            for gi, (w, d) in enumerate(ATTN_PATTERNS):
                q, k, v = (outs[part * N_GROUPS + gi].reshape(bsz * d, seq // d, GROUP_W)
                           for part in range(3))
                o, lse = _attn(q, k, v, min(ATTN_TQ, seq // d))
                os_.append(o.reshape(bsz, d, seq // d, GROUP_W))
                ls_.append(lse.reshape(bsz, d, seq // d, GROUP_W))
                n_out = min(w, seq)
                kv_p[gi].append(outs[9][:, TAIL - n_out:, gi * 2 * GROUP_W:(gi + 1) * 2 * GROUP_W]
                                .reshape(bsz, n_out, 2, HEADS, HEAD_DIM))
            xp = _merge_mlp(xp, os_, ls_, wo, li, g_mlp, wu, wd, layer, tm=ROW_TILE, dils=dils)
            os_ = [sample_attn[:, :, gi, :].reshape(1, 1, nb, GROUP_W) for gi in range(N_GROUPS)]
            ls_ = [sample_attn[:, :, N_GROUPS + gi, :].reshape(1, 1, nb, GROUP_W) for gi in range(N_GROUPS)]
            xs = _merge(xs[None], os_, ls_, wo, li, tm=nb, dils=ones_sample)[0]
            xs = _mlp(xs, g_mlp, wu, wd, layer, nb)

    kv_s = [jnp.transpose(c.reshape(c.shape[0], nb, 2, HEADS, HEAD_DIM, c.shape[-1]), (0, 1, 5, 2, 3, 4))
            for c in new_caches]
    kv_p = [jnp.stack(t) for t in kv_p]
    return (xp, xs.reshape(nb, 1, D_MODEL), jnp.stack(pool_p), jnp.stack(pool_s),
            kv_p[0], kv_s[0], kv_p[1], kv_s[1], kv_p[2], kv_s[2])
```

```python
import functools

import jax
import jax.numpy as jnp
import numpy as np
from jax import lax
from jax.experimental import pallas as pl
from jax.experimental.pallas import tpu as pltpu

F32 = jnp.float32
BF16 = jnp.bfloat16

D_MODEL = 1024
D_FF = 4 * D_MODEL
DEPTH = 4
POOL_WINDOWS = (2, 4, 8, 16)
POOL_GC = D_MODEL // len(POOL_WINDOWS)
POOL_STATE = max(POOL_WINDOWS) - 1
ATTN_PATTERNS = ((128, 1), (512, 4), (2048, 16))
N_GROUPS = len(ATTN_PATTERNS)
HEAD_DIM = 64
HEADS = 8
GROUP_W = HEADS * HEAD_DIM
QKV_W = 3 * N_GROUPS * GROUP_W
N_KEYS_M1 = 128
ROPE_THETA = 10000.0
PAST_LEN = 8192
EPS = 1e-6
NEG = -1e30
LOG2E = 1.4426950408889634
Q_SCALE = HEAD_DIM ** -0.5 * LOG2E

LANES = 128
MXU_DIM = 256
VMEM_LIMIT = 56 * 1024 * 1024
ROW_TILE = 512
POOL_TILE = 256
FF_CHUNK = 512
DOWN_CHUNK = 256
ATTN_TQ = 512
Q_SUB = 128
ATTN_STAGES = 3


def _params(*sem):
    return pltpu.CompilerParams(dimension_semantics=sem, vmem_limit_bytes=VMEM_LIMIT)


def _rmsnorm(x, g):
    return x * lax.rsqrt(jnp.mean(x * x, axis=-1, keepdims=True) + EPS) * g


def _mlp_residual(x, g, wu_ref, wd_ref, a_ref, o_ref, side=()):
    h = _rmsnorm(x, g).astype(BF16)
    n_up = D_FF // FF_CHUNK
    n_down = D_MODEL // DOWN_CHUNK
    slots = n_up + n_down - 1

    def run_side(slot):
        for j, piece in enumerate(side):
            if j * slots // len(side) == slot:
                piece()

    for c in range(n_up):
        cols = slice(c * FF_CHUNK, (c + 1) * FF_CHUNK)
        u = jnp.dot(h, wu_ref[:, cols], preferred_element_type=F32)
        a_ref[:, cols] = jnp.square(jnp.maximum(u, 0.0)).astype(BF16)
        run_side(c)
    for c in range(n_down):
        cols = slice(c * DOWN_CHUNK, (c + 1) * DOWN_CHUNK)
        o_ref[:, cols] = x[:, cols] + jnp.dot(a_ref[...], wd_ref[:, cols], preferred_element_type=F32)
        if c < n_down - 1:
            run_side(n_up + c)


def _mlp_kernel(x_ref, g_ref, wu_ref, wd_ref, o_ref, a_ref):
    _mlp_residual(x_ref[...], g_ref[...], wu_ref, wd_ref, a_ref, o_ref)


def _mlp(x, g, wu, wd, layer, tm):
    n = x.shape[0]
    return pl.pallas_call(
        _mlp_kernel,
        grid=(n // tm,),
        in_specs=[
            pl.BlockSpec((tm, D_MODEL), lambda i: (i, 0)),
            pl.BlockSpec((1, D_MODEL), lambda i: (0, 0)),
            pl.BlockSpec((None, D_MODEL, D_FF), lambda i: (layer, 0, 0)),
            pl.BlockSpec((None, D_FF, D_MODEL), lambda i: (layer, 0, 0)),
        ],
        out_specs=pl.BlockSpec((tm, D_MODEL), lambda i: (i, 0)),
        out_shape=jax.ShapeDtypeStruct((n, D_MODEL), F32),
        scratch_shapes=[pltpu.VMEM((tm, D_FF), BF16)],
        compiler_params=_params("arbitrary"),
        name="mlp",
    )(x, g, wu, wd)


def _skew_specs(n_tiles, tm):
    x_spec = pl.BlockSpec((tm, D_MODEL), lambda i: (jnp.minimum(i, n_tiles - 1), 0))
    o_spec = pl.BlockSpec((tm, D_MODEL), lambda i: (jnp.maximum(i - 1, 0), 0))
    return x_spec, o_spec


def _skew_scratch(tm):
    return [pltpu.VMEM((tm, D_MODEL), F32), pltpu.VMEM((tm, D_MODEL), F32), pltpu.VMEM((tm, D_FF), BF16)]


def _mlp_specs(layer):
    return [pl.BlockSpec((1, D_MODEL), lambda i: (0, 0)),
            pl.BlockSpec((None, D_MODEL, D_FF), lambda i: (layer, 0, 0)),
            pl.BlockSpec((None, D_FF, D_MODEL), lambda i: (layer, 0, 0))]


def _pool_mlp_kernel(x_ref, gmix_ref, w_ref, sc_ref, gmlp_ref, wu_ref, wd_ref, *rest,
                     tm, tiles_per_seq, n_tiles, buf_lens, aliased):
    ng = len(buf_lens)
    c_refs, (qkn_ref, kvt_ref) = rest[:ng], rest[ng:ng + 2]
    outs = rest[ng + 2 + (ng if aliased else 0):]
    o_ref, st_ref = outs[:2]
    co_refs, ol_ref = outs[2:2 + ng], outs[2 + ng]
    ext_ref, mid_next, mid_cur, a_ref = outs[3 + ng:]
    i = pl.program_id(0)
    halo = POOL_STATE + 1

    @pl.when(i == 0)
    def _():
        ext_ref[0:halo, :] = jnp.zeros((halo, D_MODEL), F32)
        mid_cur[...] = jnp.zeros((tm, D_MODEL), F32)

    s = jnp.minimum(i, n_tiles - 1) % tiles_per_seq
    ext_ref[0:halo, :] = jnp.where(s == 0, 0.0, ext_ref[0:halo, :])

    def normalise():
        ext_ref[halo:halo + tm, :] = _rmsnorm(x_ref[...], gmix_ref[...])

    def pool_block(gi, r0, nr):
        w = POOL_WINDOWS[gi]
        cols = slice(gi * POOL_GC, (gi + 1) * POOL_GC)
        pos = s * tm + r0 + lax.broadcasted_iota(jnp.int32, (nr, 1), 0)
        cur = ext_ref[halo + r0:halo + r0 + nr, cols]
        acc = cur
        for k in range(1, w):
            acc = acc + ext_ref[halo + r0 - k:halo + r0 - k + nr, cols]
        cnt = jnp.minimum(pos + 1, w).astype(F32)
        d = (acc / cnt - cur).astype(BF16)
        y = jnp.dot(d, w_ref[gi], preferred_element_type=F32)
        mid_next[r0:r0 + nr, cols] = x_ref[r0:r0 + nr, cols] + y * sc_ref[:, cols]

    t = jnp.minimum(i, n_tiles - 1)
    sample, half = t // 2, t % 2

    def buffer_group(g):
        kt, vt = c_refs[g][0, 0, 0], c_refs[g][0, 0, 1]
        lane = lax.broadcasted_iota(jnp.int32, kvt_ref.shape[3:], 1)
        new_k = jnp.sum(jnp.where(lane == sample, kvt_ref[half, g, 0], 0.0), axis=1, keepdims=True)
        new_v = jnp.sum(jnp.where(lane == sample, kvt_ref[half, g, 1], 0.0), axis=1, keepdims=True)
        rk, rv, o_row, lse_row = _sample_attend_and_roll(
            kt, vt, qkn_ref[0, 0, g:g + 1, :], qkn_ref[0, 0, ng + g:ng + g + 1, :],
            qkn_ref[0, 0, 2 * ng + g:2 * ng + g + 1, :], new_k, new_v, buf_lens[g], ATTN_PATTERNS[g][1])
        co_refs[g][0, 0, 0] = rk
        co_refs[g][0, 0, 1] = rv
        ol_ref[0, 0, g:g + 1, :] = o_row
        ol_ref[0, 0, ng + g:ng + g + 1, :] = lse_row

    side = [normalise]
    for gi, splits in ((3, 4), (2, 2), (1, 1), (0, 1)):
        nr = tm // splits
        side += [functools.partial(pool_block, gi, p * nr, nr) for p in range(splits)]
    for g in range(ng):
        side.insert(1 + 3 * g, functools.partial(buffer_group, ng - 1 - g))
    _mlp_residual(mid_cur[...], gmlp_ref[...], wu_ref, wd_ref, a_ref, o_ref, side)

    st_ref[0] = ext_ref[tm + 1:tm + halo, :]
    ext_ref[0:halo, :] = ext_ref[tm:tm + halo, :]
    mid_cur[...] = mid_next[...]


def _pool_mlp(x, gmix, w, sc, gmlp, wu, wd, layer, tm, caches, li, prev_rolled, qkn, kvt):
    b, s, _ = x.shape
    tps = s // tm
    n_tiles = b * tps
    nb = qkn.shape[0]
    hw = GROUP_W // 2
    assert n_tiles == 2 * nb, "one half of one sample's heads rides along with every row tile"
    ng = len(caches)
    buf_lens = tuple(c.shape[-1] for c in caches)
    aliased = prev_rolled is not None
    x_spec, o_spec = _skew_specs(n_tiles, tm)
    const = lambda i: (0, 0)

    def half_sample(i):
        t = jnp.minimum(i, n_tiles - 1)
        return t // 2, t % 2

    def cache_map(i):
        smp, half = half_sample(i)
        return (li, smp, 0, half, 0)

    def row_map(i):
        smp, half = half_sample(i)
        return (smp, half, 0, 0)

    cblk = [pl.BlockSpec((1, 1, 2, hw, n), cache_map) for n in buf_lens]
    in_specs = [
        x_spec,
        pl.BlockSpec((1, D_MODEL), const),
        pl.BlockSpec((len(POOL_WINDOWS), POOL_GC, POOL_GC), lambda i: (0, 0, 0)),
        pl.BlockSpec((1, D_MODEL), const),
    ] + _mlp_specs(layer) + cblk + [
        pl.BlockSpec((1, 1, 3 * ng, hw), row_map),
        pl.BlockSpec(kvt.shape, lambda i: (0, 0, 0, 0, 0)),
    ]
    args = [x.reshape(b * s, D_MODEL), gmix, w, sc, gmlp, wu, wd, *caches, qkn, kvt]
    aliases = {}
    if aliased:
        aliases = {len(args) + g: 2 + g for g in range(ng)}
        in_specs += [pl.BlockSpec(memory_space=pl.ANY)] * ng
        args += list(prev_rolled)
    res = pl.pallas_call(
        functools.partial(_pool_mlp_kernel, tm=tm, tiles_per_seq=tps, n_tiles=n_tiles,
                          buf_lens=buf_lens, aliased=aliased),
        grid=(n_tiles + 1,),
        in_specs=in_specs,
        out_specs=[
            o_spec,
            pl.BlockSpec((1, POOL_STATE, D_MODEL),
                         lambda i: (jnp.minimum(i, n_tiles - 1) // tps, 0, 0)),
        ] + cblk + [pl.BlockSpec((1, 1, 2 * ng, hw), row_map)],
        out_shape=[
            jax.ShapeDtypeStruct((b * s, D_MODEL), F32),
            jax.ShapeDtypeStruct((b, POOL_STATE, D_MODEL), F32),
        ] + [jax.ShapeDtypeStruct(c.shape, F32) for c in caches]
        + [jax.ShapeDtypeStruct((nb, 2, 2 * ng, hw), F32)],
        input_output_aliases=aliases,
        scratch_shapes=[pltpu.VMEM((tm + POOL_STATE + 1, D_MODEL), F32)] + _skew_scratch(tm),
        compiler_params=_params("arbitrary"),
        name="pool_mlp",
    )(*args)
    return res[0].reshape(b, s, D_MODEL), res[1], list(res[2:2 + ng]), res[2 + ng]


def _pool_sample_kernel(x_ref, st_ref, g_ref, w_ref, sc_ref, o_ref, nst_ref, *, pos):
    x = x_ref[...]
    xn = _rmsnorm(x, g_ref[...])
    for gi, w in enumerate(POOL_WINDOWS):
        cols = slice(gi * POOL_GC, (gi + 1) * POOL_GC)
        cur = xn[:, cols]
        acc = cur
        for k in range(1, w):
            acc = acc + st_ref[POOL_STATE - k, :, cols]
        d = acc / float(min(pos + 1, w)) - cur
        y = jnp.dot(d.astype(BF16), w_ref[gi], preferred_element_type=F32)
        o_ref[:, cols] = x[:, cols] + y * sc_ref[:, cols]
    for k in range(POOL_STATE - 1):
        nst_ref[k] = st_ref[k + 1]
    nst_ref[POOL_STATE - 1] = xn


def _pool_sample(x, st, g, w, sc, pos):
    n = x.shape[0]
    return pl.pallas_call(
        functools.partial(_pool_sample_kernel, pos=pos),
        out_shape=[
            jax.ShapeDtypeStruct((n, D_MODEL), F32),
            jax.ShapeDtypeStruct((POOL_STATE, n, D_MODEL), F32),
        ],
        compiler_params=pltpu.CompilerParams(vmem_limit_bytes=VMEM_LIMIT),
        name="pool_sample",
    )(x, st, g, w, sc)


def _rope_tables(pos):
    half = HEAD_DIM // 2
    inv = ROPE_THETA ** (-np.arange(half, dtype=np.float64) * 2.0 / HEAD_DIM)
    ang = pos.astype(np.float64)[:, None] * inv[None, :]
    cos, sin = np.cos(ang), np.sin(ang)
    return (jnp.asarray(np.concatenate([cos, cos, cos, cos], axis=1), F32),
            jnp.asarray(np.concatenate([-sin, sin, -sin, sin], axis=1), F32))


def _gain_tables(gain):
    half = HEAD_DIM // 2
    swapped = jnp.concatenate([gain[:, half:], gain[:, :half]], axis=1)
    return jnp.concatenate([gain, gain], axis=1), jnp.concatenate([swapped, swapped], axis=1)


def _qkv_kernel(x_ref, g_ref, w_ref, gq_ref, gqs_ref, gk_ref, gks_ref, cos_ref, sin_ref, ones_ref,
                *rest, tm, dils, out_dtype, tail_lens, n_prev):
    rest = rest[n_prev:]
    outs = rest[:9]
    tails = rest[9:9 + len(tail_lens)]
    stage_ref = rest[9 + len(tail_lens)]
    kv_stage = rest[-1] if tail_lens else None
    x = x_ref[0]
    h = _rmsnorm(x, g_ref[...]).astype(BF16)
    cosf = cos_ref[...]
    sinf = sin_ref[...]
    lane = lax.broadcasted_iota(jnp.int32, (tm, LANES), 1)
    first_half = (lane % HEAD_DIM) < (HEAD_DIM // 2)
    for part in range(3):
        for gi in range(N_GROUPS):
            c0 = (part * N_GROUPS + gi) * GROUP_W
            res = jnp.dot(h, w_ref[:, c0:c0 + GROUP_W], preferred_element_type=F32)
            if part < 2:
                gain = (gq_ref, gk_ref)[part][gi:gi + 1, :]
                gain_sw = (gqs_ref, gks_ref)[part][gi:gi + 1, :]
                ta = gain * cosf
                tb = gain_sw * sinf
                if part == 0:
                    ta = ta * Q_SCALE
                    tb = tb * Q_SCALE
                sq = (res * res).astype(BF16)
                pieces = []
                for hh in range(GROUP_W // MXU_DIM):
                    ss = jnp.dot(sq[:, hh * MXU_DIM:(hh + 1) * MXU_DIM], ones_ref[...],
                                 preferred_element_type=F32)
                    r = lax.rsqrt(ss * (1.0 / HEAD_DIM) + EPS)
                    for sl in range(MXU_DIM // LANES):
                        l0 = hh * MXU_DIM + sl * LANES
                        xs = res[:, l0:l0 + LANES]
                        swapped = jnp.where(first_half,
                                            pltpu.roll(xs, LANES - HEAD_DIM // 2, 1),
                                            pltpu.roll(xs, HEAD_DIM // 2, 1))
                        pieces.append(r[:, sl * LANES:(sl + 1) * LANES] * (xs * ta + swapped * tb))
                res = jnp.concatenate(pieces, axis=1)
            if tail_lens and part > 0:
                t0 = (gi * 2 + part - 1) * GROUP_W
                kv_stage[:, t0:t0 + GROUP_W] = res
            out_ref = outs[part * N_GROUPS + gi]
            d = dils[gi]
            if d == 1:
                out_ref[0, 0] = res.astype(out_dtype)
            else:
                for c in range(GROUP_W // LANES):
                    stage_ref[c] = res[:, c * LANES:(c + 1) * LANES]
                for r_ in range(d):
                    for c in range(GROUP_W // LANES):
                        out_ref[0, r_, :, c * LANES:(c + 1) * LANES] = (
                            stage_ref[c, pl.ds(r_, tm // d, stride=d), :].astype(out_dtype))

    j = pl.program_id(1)
    for gi, n in enumerate(tail_lens):
        rows = min(n, tm)

        @pl.when(j >= pl.num_programs(1) - max(n // tm, 1))
        def _(gi=gi, rows=rows):
            for part in range(2):
                t0 = (gi * 2 + part) * GROUP_W
                tails[gi][0, 0, part * GROUP_W:(part + 1) * GROUP_W, :] = (
                    kv_stage[tm - rows:tm, t0:t0 + GROUP_W].T)


def _qkv(x, g, w, li, gq, gk, cosf, sinf, ones, *, tm, dils, out_dtype, tail_lens=(), prev_tails=None):
    b, s, _ = x.shape
    nt = s // tm
    gq_t, gq_s = gq
    gk_t, gk_s = gk
    const2 = lambda i, j: (0, 0)
    in_specs = [
        pl.BlockSpec((1, tm, D_MODEL), lambda i, j: (i, j, 0)),
        pl.BlockSpec((1, D_MODEL), const2),
        pl.BlockSpec((None, D_MODEL, QKV_W), lambda i, j: (li, 0, 0)),
        pl.BlockSpec((N_GROUPS, LANES), const2),
        pl.BlockSpec((N_GROUPS, LANES), const2),
        pl.BlockSpec((N_GROUPS, LANES), const2),
        pl.BlockSpec((N_GROUPS, LANES), const2),
        pl.BlockSpec((tm, LANES), lambda i, j: (j, 0)),
        pl.BlockSpec((tm, LANES), lambda i, j: (j, 0)),
        pl.BlockSpec((MXU_DIM, MXU_DIM), const2),
    ]
    out_specs, out_shape = [], []
    for _ in range(3):
        for d in dils:
            out_specs.append(pl.BlockSpec((1, d, tm // d, GROUP_W), lambda i, j: (i, 0, j, 0)))
            out_shape.append(jax.ShapeDtypeStruct((b, d, s // d, GROUP_W), out_dtype))
    args = [x, g, w, gq_t, gq_s, gk_t, gk_s, cosf, sinf, ones]
    aliases = {}
    if prev_tails is not None:
        aliases = {len(args) + gi: 9 + gi for gi in range(len(tail_lens))}
        in_specs += [pl.BlockSpec(memory_space=pl.ANY)] * len(tail_lens)
        args += list(prev_tails)
    for n in tail_lens:
        first = nt - max(n // tm, 1)
        out_specs.append(pl.BlockSpec((1, 1, 2 * GROUP_W, min(n, tm)),
                                      lambda i, j, first=first: (li, i, 0, jnp.maximum(j - first, 0))))
        out_shape.append(jax.ShapeDtypeStruct((w.shape[0], b, 2 * GROUP_W, n), F32))
    scratch = [pltpu.VMEM((GROUP_W // LANES, tm, LANES), F32)]
    if tail_lens:
        scratch.append(pltpu.VMEM((tm, 2 * N_GROUPS * GROUP_W), F32))
    return pl.pallas_call(
        functools.partial(_qkv_kernel, tm=tm, dils=dils, out_dtype=out_dtype, tail_lens=tuple(tail_lens),
                          n_prev=len(args) - 10),
        grid=(b, nt),
        in_specs=in_specs,
        out_specs=out_specs,
        out_shape=out_shape,
        input_output_aliases=aliases,
        scratch_shapes=scratch,
        compiler_params=_params("arbitrary", "arbitrary"),
        name="qkv",
    )(*args)


def _stat_is_max(shape):
    lane = lax.broadcasted_iota(jnp.int32, shape, len(shape) - 1)
    return (lane % HEAD_DIM) < (HEAD_DIM // 2)


def _unpack_stats(packed):
    is_max = _stat_is_max(packed.shape)
    half = HEAD_DIM // 2
    m = jnp.where(is_max, packed, pltpu.roll(packed, half, 1))
    l = jnp.where(is_max, pltpu.roll(packed, LANES - half, 1), packed)
    return m, l


def _attn_kernel(q_ref, kp_ref, kc_ref, vp_ref, vc_ref, o_ref, ml_ref, kcat_ref, vcat_ref, p_scr, m_scr,
                 *, tq):
    j = pl.program_id(1)
    kcat_ref[0:Q_SUB, :] = kp_ref[0]
    kcat_ref[Q_SUB:Q_SUB + tq, :] = kc_ref[0]
    vcat_ref[0:Q_SUB, :] = vp_ref[0]
    vcat_ref[Q_SUB:Q_SUB + tq, :] = vc_ref[0]

    nk = 2 * Q_SUB
    nq = 2 * Q_SUB
    row = lax.broadcasted_iota(jnp.int32, (nq, nk), 0) & (Q_SUB - 1)
    col = lax.broadcasted_iota(jnp.int32, (nq, nk), 1)
    dist = row + Q_SUB - col
    band = (dist >= 0) & (dist <= N_KEYS_M1)
    lane = lax.broadcasted_iota(jnp.int32, (Q_SUB, LANES), 1)
    low_head = lane < HEAD_DIM
    stat_is_max = _stat_is_max((nq, LANES))
    zero_q = jnp.zeros((Q_SUB, LANES), BF16)
    ones_v = jnp.ones((nk, LANES), BF16)

    first_valid = band & ((col >= Q_SUB) | (j > 0))
    jobs = [(sb * Q_SUB, slice(p * LANES, (p + 1) * LANES))
            for sb in range(tq // Q_SUB) for p in range(GROUP_W // LANES)]

    def probabilities(idx):
        r0, lanes = jobs[idx]
        qp = q_ref[0, r0:r0 + Q_SUB, lanes]
        q2 = jnp.concatenate([jnp.where(low_head, qp, zero_q), jnp.where(low_head, zero_q, qp)], axis=0)
        s = lax.dot_general(q2, kcat_ref[r0:r0 + nk, lanes], (((1,), (1,)), ((), ())),
                            preferred_element_type=F32)
        s = jnp.where(first_valid if r0 == 0 else band, s, NEG)
        m = jnp.max(s, axis=1, keepdims=True)
        p_scr[idx % ATTN_STAGES] = jnp.exp2(s - m).astype(BF16)
        m_scr[idx % ATTN_STAGES] = jnp.broadcast_to(m, (nq, LANES))

    def outputs(idx):
        r0, lanes = jobs[idx]
        vaug = jnp.concatenate([vcat_ref[r0:r0 + nk, lanes], ones_v], axis=1)
        ov = jnp.dot(p_scr[idx % ATTN_STAGES], vaug, preferred_element_type=F32)
        num = ov[:, :LANES]
        ml = jnp.where(stat_is_max, m_scr[idx % ATTN_STAGES], ov[:, LANES:])
        o_ref[0, r0:r0 + Q_SUB, lanes] = jnp.where(low_head, num[:Q_SUB], num[Q_SUB:])
        ml_ref[0, r0:r0 + Q_SUB, lanes] = jnp.where(low_head, ml[:Q_SUB], ml[Q_SUB:])

    lag = ATTN_STAGES - 1
    for idx in range(len(jobs) + lag):
        if idx < len(jobs):
            probabilities(idx)
        if idx >= lag:
            outputs(idx - lag)


def _attn(q, k, v, tq):
    nseq, length, _ = q.shape
    ratio = tq // Q_SUB
    cur = lambda i, j: (i, j, 0)
    prev = lambda i, j: (i, jnp.maximum(j * ratio - 1, 0), 0)
    return pl.pallas_call(
        functools.partial(_attn_kernel, tq=tq),
        grid=(nseq, length // tq),
        in_specs=[
            pl.BlockSpec((1, tq, GROUP_W), cur),
            pl.BlockSpec((1, Q_SUB, GROUP_W), prev),
            pl.BlockSpec((1, tq, GROUP_W), cur),
            pl.BlockSpec((1, Q_SUB, GROUP_W), prev),
            pl.BlockSpec((1, tq, GROUP_W), cur),
        ],
        out_specs=[pl.BlockSpec((1, tq, GROUP_W), cur), pl.BlockSpec((1, tq, GROUP_W), cur)],
        out_shape=[jax.ShapeDtypeStruct((nseq, length, GROUP_W), F32)] * 2,
        scratch_shapes=[pltpu.VMEM((tq + Q_SUB, GROUP_W), BF16)] * 2
        + [pltpu.VMEM((ATTN_STAGES, 2 * Q_SUB, 2 * Q_SUB), BF16),
           pltpu.VMEM((ATTN_STAGES, 2 * Q_SUB, LANES), F32)],
        compiler_params=_params("arbitrary", "arbitrary"),
        name="attn_prompt",
    )(q, k, k, v, v)


def _merge_pieces(load_x, store_y, o_refs, l_refs, wo_ref, und, tm, dils):
    nc = GROUP_W // LANES

    def slabs(ref, d, scr, r0, nr):
        if d == 1:
            return [ref[0, 0, r0:r0 + nr, c * LANES:(c + 1) * LANES] for c in range(nc)]
        for r_ in range(d):
            for c in range(nc):
                scr[c, pl.ds(r0 + r_, nr // d, stride=d), :] = (
                    ref[0, r_, r0 // d:(r0 + nr) // d, c * LANES:(c + 1) * LANES])
        return [scr[c, r0:r0 + nr, :] for c in range(nc)]

    def merge_rows(r0, nr):
        os_ = [slabs(o_refs[g], dils[g], und[2 * g], r0, nr) for g in range(N_GROUPS)]
        ss_ = [slabs(l_refs[g], dils[g], und[2 * g + 1], r0, nr) for g in range(N_GROUPS)]
        merged = []
        for c in range(nc):
            ms, ls_ = zip(*[_unpack_stats(ss_[g][c]) for g in range(N_GROUPS)])
            mx = jnp.maximum(jnp.maximum(ms[0], ms[1]), ms[2])
            ws = [jnp.exp2(m - mx) for m in ms]
            num = ws[0] * os_[0][c] + ws[1] * os_[1][c] + ws[2] * os_[2][c]
            den = ws[0] * ls_[0] + ws[1] * ls_[1] + ws[2] * ls_[2]
            merged.append((num / den).astype(BF16))
        merged = jnp.concatenate(merged, axis=1)
        store_y(r0, nr, load_x(r0, nr) + jnp.dot(merged, wo_ref[...], preferred_element_type=F32))

    nr = min(tm, Q_SUB)
    return [functools.partial(merge_rows, r0, nr) for r0 in range(0, tm, nr)]


def _merge_scratch(tm):
    return [pltpu.VMEM((GROUP_W // LANES, tm, LANES), F32)] * (2 * N_GROUPS)


def _merge_kernel(x_ref, o0, o1, o2, l0, l1, l2, wo_ref, y_ref, *scratch, tm, dils):
    def store_y(r0, nr, v):
        y_ref[0, r0:r0 + nr, :] = v

    for piece in _merge_pieces(lambda r0, nr: x_ref[0, r0:r0 + nr, :], store_y, (o0, o1, o2),
                               (l0, l1, l2), wo_ref, scratch, tm, dils):
        piece()


def _merge_mlp_kernel(x_ref, o0, o1, o2, l0, l1, l2, wo_ref, gmlp_ref, wu_ref, wd_ref, y_ref,
                      *scratch, tm, dils):
    und = scratch[:2 * N_GROUPS]
    mid_next, mid_cur, a_ref = scratch[2 * N_GROUPS:]

    @pl.when(pl.program_id(0) == 0)
    def _():
        mid_cur[...] = jnp.zeros((tm, D_MODEL), F32)

    def store_mid(r0, nr, v):
        mid_next[r0:r0 + nr, :] = v

    side = _merge_pieces(lambda r0, nr: x_ref[r0:r0 + nr, :], store_mid, (o0, o1, o2), (l0, l1, l2),
                         wo_ref, und, tm, dils)
    _mlp_residual(mid_cur[...], gmlp_ref[...], wu_ref, wd_ref, a_ref, y_ref, side)
    mid_cur[...] = mid_next[...]


def _merge_mlp(x, os_, ls_, wo, li, gmlp, wu, wd, layer, *, tm, dils):
    b, s, _ = x.shape
    tps = s // tm
    n_tiles = b * tps
    x_spec, o_spec = _skew_specs(n_tiles, tm)

    def grp_map(i):
        t = jnp.minimum(i, n_tiles - 1)
        return (t // tps, 0, t % tps, 0)

    grp = [pl.BlockSpec((1, d, tm // d, GROUP_W), grp_map) for d in dils]
    y = pl.pallas_call(
        functools.partial(_merge_mlp_kernel, tm=tm, dils=dils),
        grid=(n_tiles + 1,),
        in_specs=[x_spec] + grp + grp
        + [pl.BlockSpec((None, GROUP_W, D_MODEL), lambda i: (li, 0, 0))] + _mlp_specs(layer),
        out_specs=o_spec,
        out_shape=jax.ShapeDtypeStruct((b * s, D_MODEL), F32),
        scratch_shapes=_merge_scratch(tm) + _skew_scratch(tm),
        compiler_params=_params("arbitrary"),
        name="merge_mlp",
    )(x.reshape(b * s, D_MODEL), *os_, *ls_, wo, gmlp, wu, wd)
    return y.reshape(b, s, D_MODEL)


def _merge(x, os_, ls_, wo, li, *, tm, dils):
    b, s, _ = x.shape
    grp = [pl.BlockSpec((1, d, tm // d, GROUP_W), lambda i, j: (i, 0, j, 0)) for d in dils]
    return pl.pallas_call(
        functools.partial(_merge_kernel, tm=tm, dils=dils),
        grid=(b, s // tm),
        in_specs=[pl.BlockSpec((1, tm, D_MODEL), lambda i, j: (i, j, 0))] + grp + grp
        + [pl.BlockSpec((None, GROUP_W, D_MODEL), lambda i, j: (li, 0, 0))],
        out_specs=pl.BlockSpec((1, tm, D_MODEL), lambda i, j: (i, j, 0)),
        out_shape=jax.ShapeDtypeStruct((b, s, D_MODEL), F32),
        scratch_shapes=_merge_scratch(tm),
        compiler_params=_params("arbitrary", "arbitrary"),
        name="merge_out_proj",
    )(x, *os_, *ls_, wo)


def _sample_attend_and_roll(kt, vt, q_row, kn_row, vn_row, new_k, new_v, n, dil):
    hw = kt.shape[0]
    nh = hw // HEAD_DIM
    sub = lax.broadcasted_iota(jnp.int32, (nh, hw), 0)
    ln = lax.broadcasted_iota(jnp.int32, (nh, hw), 1)
    own = (ln // HEAD_DIM) == sub
    e = lax.broadcasted_iota(jnp.int32, (nh, n), 1)
    qb = jnp.where(own, jnp.broadcast_to(q_row, (nh, hw)), 0.0)
    s = jnp.dot(qb.astype(BF16), kt.astype(BF16), preferred_element_type=F32)
    s = jnp.where((n - e) % dil == 0, s, NEG)
    s_new = jnp.sum(qb * kn_row, axis=1, keepdims=True)
    m = jnp.maximum(jnp.max(s, axis=1, keepdims=True), s_new)
    pr = jnp.exp2(s - m)
    p_new = jnp.exp2(s_new - m)
    l = jnp.sum(pr, axis=1, keepdims=True) + p_new
    o = lax.dot_general(pr.astype(BF16), vt.astype(BF16), (((1,), (1,)), ((), ())),
                        preferred_element_type=F32)
    o = o + p_new * vn_row
    o_row = jnp.sum(jnp.where(own, o, 0.0), axis=0, keepdims=True)
    stats = jnp.where(_stat_is_max((nh, hw)), m, l)
    stats_row = jnp.sum(jnp.where(own, stats, 0.0), axis=0, keepdims=True)
    return _shift_in(kt, new_k), _shift_in(vt, new_v), o_row, stats_row


def _shift_in(x, new_col):
    rows, n = x.shape
    lane0 = lax.broadcasted_iota(jnp.int32, (rows, LANES), 1) == 0
    out = []
    for j in range(n // LANES):
        cur = x[:, j * LANES:(j + 1) * LANES]
        nxt = x[:, (j + 1) * LANES:(j + 2) * LANES] if (j + 1) * LANES < n else new_col
        out.append(pltpu.roll(jnp.where(lane0, nxt, cur), LANES - 1, 1))
    return jnp.concatenate(out, axis=1)


def kernel(x_prompt, x_sample, state_pool, cache_kv_w128, cache_kv_w512, cache_kv_w2048, norm_mix,
           norm_mlp, pool_w, pool_scale, attn_w_qkv, attn_q_norm, attn_k_norm, attn_w_o, mlp_w_up,
           mlp_w_down):
    bsz, seq, _ = x_prompt.shape
    nb = x_sample.shape[0]
    past = PAST_LEN
    dils = tuple(d for _, d in ATTN_PATTERNS)
    ones_sample = (1, 1, 1)

    wu = mlp_w_up.astype(BF16)
    wd = mlp_w_down.astype(BF16)
    wqkv = attn_w_qkv.astype(BF16)
    wo = attn_w_o.astype(BF16)
    wp = pool_w.astype(BF16)
    head = lax.broadcasted_iota(jnp.int32, (MXU_DIM, MXU_DIM), 0) // HEAD_DIM
    ones = (head == head.T).astype(BF16)
    cos_p, sin_p = _rope_tables(np.arange(seq))
    cos_s, sin_s = _rope_tables(np.full((nb,), past))

    caches = [jnp.transpose(c, (0, 1, 3, 4, 5, 2)).reshape(c.shape[0], nb, 2, GROUP_W, c.shape[2])
              for c in (cache_kv_w128, cache_kv_w512, cache_kv_w2048)]

    xp = x_prompt
    xs = x_sample.reshape(nb, D_MODEL)
    pool_p, pool_s = [], []
    new_caches = kv_tails = None
    hw = GROUP_W // 2
    for layer in range(DEPTH):
        li = layer // 2
        g_mix = norm_mix[layer][None, :]
        g_mlp = norm_mlp[layer][None, :]
        gq = _gain_tables(attn_q_norm[li])
        gk = _gain_tables(attn_k_norm[li])
        if layer % 2 == 0:
            sc = pool_scale[li][None, :]
            st_in = jnp.transpose(state_pool[li], (1, 0, 2))
            xs, st = _pool_sample(xs, st_in, g_mix, wp[li], sc, past)
            pool_s.append(jnp.transpose(st, (1, 0, 2)))
            xs = _mlp(xs, g_mlp, wu, wd, layer, nb)
            outs = _qkv(xs[None], norm_mix[layer + 1][None, :], wqkv, li, gq, gk, cos_s, sin_s, ones,
                        tm=nb, dils=ones_sample, out_dtype=F32)
            rows = jnp.stack([o[0, 0] for o in outs[:9]], axis=1)
            qkn = jnp.transpose(rows.reshape(nb, 9, 2, hw), (0, 2, 1, 3))
            kvt = jnp.transpose(rows[:, N_GROUPS:].reshape(nb, 2, N_GROUPS, 2, hw), (3, 2, 1, 4, 0))
            xp, st, new_caches, sample_attn = _pool_mlp(xp, g_mix, wp[li], sc, g_mlp, wu, wd, layer,
                                                        POOL_TILE, caches, li, new_caches, qkn, kvt)
            pool_p.append(st)
        else:
            outs = _qkv(xp, g_mix, wqkv, li, gq, gk, cos_p, sin_p, ones, tm=ROW_TILE, dils=dils,
                        out_dtype=BF16, tail_lens=tuple(min(w, seq) for w, _ in ATTN_PATTERNS),
                        prev_tails=kv_tails)
            kv_tails = outs[9:]
            os_, ls_ = [], []
            for gi, (w, d) in enumerate(ATTN_PATTERNS):
                q, k, v = (outs[part * N_GROUPS + gi].reshape(bsz * d, seq // d, GROUP_W)
                           for part in range(3))
                o, stats = _attn(q, k, v, min(ATTN_TQ, seq // d))
                os_.append(o.reshape(bsz, d, seq // d, GROUP_W))
                ls_.append(stats.reshape(bsz, d, seq // d, GROUP_W))
            xp = _merge_mlp(xp, os_, ls_, wo, li, g_mlp, wu, wd, layer, tm=ROW_TILE, dils=dils)
            os_ = [sample_attn[:, :, gi, :].reshape(1, 1, nb, GROUP_W) for gi in range(N_GROUPS)]
            ls_ = [sample_attn[:, :, N_GROUPS + gi, :].reshape(1, 1, nb, GROUP_W) for gi in range(N_GROUPS)]
            xs = _merge(xs[None], os_, ls_, wo, li, tm=nb, dils=ones_sample)[0]
            xs = _mlp(xs, g_mlp, wu, wd, layer, nb)

    kv_s, kv_p = ([jnp.transpose(c.reshape(c.shape[0], c.shape[1], 2, HEADS, HEAD_DIM, c.shape[-1]),
                                 (0, 1, 5, 2, 3, 4)) for c in group] for group in (new_caches, kv_tails))
    return (xp, xs.reshape(nb, 1, D_MODEL), jnp.stack(pool_p), jnp.stack(pool_s),
            kv_p[0], kv_s[0], kv_p[1], kv_s[1], kv_p[2], kv_s[2])
```

```python
import functools

import jax
import jax.numpy as jnp
import numpy as np
from jax import lax
from jax.experimental import pallas as pl
from jax.experimental.pallas import tpu as pltpu

F32 = jnp.float32
BF16 = jnp.bfloat16

D_MODEL = 1024
D_FF = 4 * D_MODEL
DEPTH = 4
POOL_WINDOWS = (2, 4, 8, 16)
POOL_GC = D_MODEL // len(POOL_WINDOWS)
POOL_STATE = max(POOL_WINDOWS) - 1
ATTN_PATTERNS = ((128, 1), (512, 4), (2048, 16))
N_GROUPS = len(ATTN_PATTERNS)
HEAD_DIM = 64
HEADS = 8
GROUP_W = HEADS * HEAD_DIM
QKV_W = 3 * N_GROUPS * GROUP_W
N_KEYS_M1 = 128
ROPE_THETA = 10000.0
PAST_LEN = 8192
EPS = 1e-6
NEG = -1e30
LOG2E = 1.4426950408889634
Q_SCALE = HEAD_DIM ** -0.5 * LOG2E

LANES = 128
MXU_DIM = 256
VMEM_LIMIT = 56 * 1024 * 1024
ROW_TILE = 512
POOL_TILE = 256
FF_CHUNK = 512
DOWN_CHUNK = 256
ATTN_TQ = 512
Q_SUB = 128
QKV_LOOKAHEAD = 1
ATTN_STAGES = 3


def _params(*sem):
    return pltpu.CompilerParams(dimension_semantics=sem, vmem_limit_bytes=VMEM_LIMIT)


def _rmsnorm(x, g):
    return x * lax.rsqrt(jnp.mean(x * x, axis=-1, keepdims=True) + EPS) * g


def _mlp_residual(x, g, wu_ref, wd_ref, a_ref, o_ref, side=(), side_offset=0):
    h = _rmsnorm(x, g).astype(BF16)
    n_up = D_FF // FF_CHUNK
    n_down = D_MODEL // DOWN_CHUNK
    slots = n_up + n_down - 1

    def run_side(slot):
        for j, piece in enumerate(side):
            if min((2 * j + side_offset) * slots // (2 * len(side)), slots - 1) == slot:
                piece()

    for c in range(n_up):
        cols = slice(c * FF_CHUNK, (c + 1) * FF_CHUNK)
        u = jnp.dot(h, wu_ref[:, cols], preferred_element_type=F32)
        a_ref[:, cols] = jnp.square(jnp.maximum(u, 0.0)).astype(BF16)
        run_side(c)
    for c in range(n_down):
        cols = slice(c * DOWN_CHUNK, (c + 1) * DOWN_CHUNK)
        o_ref[:, cols] = x[:, cols] + jnp.dot(a_ref[...], wd_ref[:, cols], preferred_element_type=F32)
        if c < n_down - 1:
            run_side(n_up + c)


def _mlp_kernel(x_ref, g_ref, wu_ref, wd_ref, o_ref, a_ref):
    _mlp_residual(x_ref[...], g_ref[...], wu_ref, wd_ref, a_ref, o_ref)


def _mlp(x, g, wu, wd, layer, tm):
    n = x.shape[0]
    return pl.pallas_call(
        _mlp_kernel,
        grid=(n // tm,),
        in_specs=[
            pl.BlockSpec((tm, D_MODEL), lambda i: (i, 0)),
            pl.BlockSpec((1, D_MODEL), lambda i: (0, 0)),
            pl.BlockSpec((None, D_MODEL, D_FF), lambda i: (layer, 0, 0)),
            pl.BlockSpec((None, D_FF, D_MODEL), lambda i: (layer, 0, 0)),
        ],
        out_specs=pl.BlockSpec((tm, D_MODEL), lambda i: (i, 0)),
        out_shape=jax.ShapeDtypeStruct((n, D_MODEL), F32),
        scratch_shapes=[pltpu.VMEM((tm, D_FF), BF16)],
        compiler_params=_params("arbitrary"),
        name="mlp",
    )(x, g, wu, wd)


def _skew_specs(n_tiles, tm):
    x_spec = pl.BlockSpec((tm, D_MODEL), lambda i: (jnp.minimum(i, n_tiles - 1), 0))
    o_spec = pl.BlockSpec((tm, D_MODEL), lambda i: (jnp.maximum(i - 1, 0), 0))
    return x_spec, o_spec


def _skew_scratch(tm):
    return [pltpu.VMEM((tm, D_MODEL), F32), pltpu.VMEM((tm, D_MODEL), F32), pltpu.VMEM((tm, D_FF), BF16)]


def _mlp_specs(layer):
    return [pl.BlockSpec((1, D_MODEL), lambda i: (0, 0)),
            pl.BlockSpec((None, D_MODEL, D_FF), lambda i: (layer, 0, 0)),
            pl.BlockSpec((None, D_FF, D_MODEL), lambda i: (layer, 0, 0))]


def _pool_mlp_kernel(x_ref, gmix_ref, w_ref, sc_ref, gmlp_ref, wu_ref, wd_ref, *rest,
                     tm, tiles_per_seq, n_tiles, buf_lens, aliased):
    ng = len(buf_lens)
    c_refs, (qkn_ref, kvt_ref) = rest[:ng], rest[ng:ng + 2]
    outs = rest[ng + 2 + (ng if aliased else 0):]
    o_ref, st_ref = outs[:2]
    co_refs, ol_ref = outs[2:2 + ng], outs[2 + ng]
    ext_ref, mid_next, mid_cur, a_ref = outs[3 + ng:]
    i = pl.program_id(0)
    halo = POOL_STATE + 1

    @pl.when(i == 0)
    def _():
        ext_ref[0:halo, :] = jnp.zeros((halo, D_MODEL), F32)
        mid_cur[...] = jnp.zeros((tm, D_MODEL), F32)

    s = jnp.minimum(i, n_tiles - 1) % tiles_per_seq
    ext_ref[0:halo, :] = jnp.where(s == 0, 0.0, ext_ref[0:halo, :])

    def normalise():
        ext_ref[halo:halo + tm, :] = _rmsnorm(x_ref[...], gmix_ref[...])

    def pool_block(gi, r0, nr):
        w = POOL_WINDOWS[gi]
        cols = slice(gi * POOL_GC, (gi + 1) * POOL_GC)
        pos = s * tm + r0 + lax.broadcasted_iota(jnp.int32, (nr, 1), 0)
        cur = ext_ref[halo + r0:halo + r0 + nr, cols]
        acc = cur
        for k in range(1, w):
            acc = acc + ext_ref[halo + r0 - k:halo + r0 - k + nr, cols]
        cnt = jnp.minimum(pos + 1, w).astype(F32)
        d = (acc / cnt - cur).astype(BF16)
        y = jnp.dot(d, w_ref[gi], preferred_element_type=F32)
        mid_next[r0:r0 + nr, cols] = x_ref[r0:r0 + nr, cols] + y * sc_ref[:, cols]

    t = jnp.minimum(i, n_tiles - 1)
    sample, half = t // 2, t % 2

    def buffer_group(g):
        kt, vt = c_refs[g][0, 0, 0], c_refs[g][0, 0, 1]
        lane = lax.broadcasted_iota(jnp.int32, kvt_ref.shape[3:], 1)
        new_k = jnp.sum(jnp.where(lane == sample, kvt_ref[half, g, 0], 0.0), axis=1, keepdims=True)
        new_v = jnp.sum(jnp.where(lane == sample, kvt_ref[half, g, 1], 0.0), axis=1, keepdims=True)
        rk, rv, o_row, lse_row = _sample_attend_and_roll(
            kt, vt, qkn_ref[0, 0, g:g + 1, :], qkn_ref[0, 0, ng + g:ng + g + 1, :],
            qkn_ref[0, 0, 2 * ng + g:2 * ng + g + 1, :], new_k, new_v, buf_lens[g], ATTN_PATTERNS[g][1])
        co_refs[g][0, 0, 0] = rk
        co_refs[g][0, 0, 1] = rv
        ol_ref[0, 0, g:g + 1, :] = o_row
        ol_ref[0, 0, ng + g:ng + g + 1, :] = lse_row

    side = [normalise]
    for gi, splits in ((3, 4), (2, 2), (1, 1), (0, 1)):
        nr = tm // splits
        side += [functools.partial(pool_block, gi, p * nr, nr) for p in range(splits)]
    for g in range(ng):
        side.insert(1 + 3 * g, functools.partial(buffer_group, ng - 1 - g))
    _mlp_residual(mid_cur[...], gmlp_ref[...], wu_ref, wd_ref, a_ref, o_ref, side)

    st_ref[0] = ext_ref[tm + 1:tm + halo, :]
    ext_ref[0:halo, :] = ext_ref[tm:tm + halo, :]
    mid_cur[...] = mid_next[...]


def _pool_mlp(x, gmix, w, sc, gmlp, wu, wd, layer, tm, caches, li, prev_rolled, qkn, kvt):
    b, s, _ = x.shape
    tps = s // tm
    n_tiles = b * tps
    nb = qkn.shape[0]
    hw = GROUP_W // 2
    assert n_tiles == 2 * nb, "one half of one sample's heads rides along with every row tile"
    ng = len(caches)
    buf_lens = tuple(c.shape[-1] for c in caches)
    aliased = prev_rolled is not None
    x_spec, o_spec = _skew_specs(n_tiles, tm)
    const = lambda i: (0, 0)

    def half_sample(i):
        t = jnp.minimum(i, n_tiles - 1)
        return t // 2, t % 2

    def cache_map(i):
        smp, half = half_sample(i)
        return (li, smp, 0, half, 0)

    def row_map(i):
        smp, half = half_sample(i)
        return (smp, half, 0, 0)

    cblk = [pl.BlockSpec((1, 1, 2, hw, n), cache_map) for n in buf_lens]
    in_specs = [
        x_spec,
        pl.BlockSpec((1, D_MODEL), const),
        pl.BlockSpec((len(POOL_WINDOWS), POOL_GC, POOL_GC), lambda i: (0, 0, 0)),
        pl.BlockSpec((1, D_MODEL), const),
    ] + _mlp_specs(layer) + cblk + [
        pl.BlockSpec((1, 1, 3 * ng, hw), row_map),
        pl.BlockSpec(kvt.shape, lambda i: (0, 0, 0, 0, 0)),
    ]
    args = [x.reshape(b * s, D_MODEL), gmix, w, sc, gmlp, wu, wd, *caches, qkn, kvt]
    aliases = {}
    if aliased:
        aliases = {len(args) + g: 2 + g for g in range(ng)}
        in_specs += [pl.BlockSpec(memory_space=pl.ANY)] * ng
        args += list(prev_rolled)
    res = pl.pallas_call(
        functools.partial(_pool_mlp_kernel, tm=tm, tiles_per_seq=tps, n_tiles=n_tiles,
                          buf_lens=buf_lens, aliased=aliased),
        grid=(n_tiles + 1,),
        in_specs=in_specs,
        out_specs=[
            o_spec,
            pl.BlockSpec((1, POOL_STATE, D_MODEL),
                         lambda i: (jnp.minimum(i, n_tiles - 1) // tps, 0, 0)),
        ] + cblk + [pl.BlockSpec((1, 1, 2 * ng, hw), row_map)],
        out_shape=[
            jax.ShapeDtypeStruct((b * s, D_MODEL), F32),
            jax.ShapeDtypeStruct((b, POOL_STATE, D_MODEL), F32),
        ] + [jax.ShapeDtypeStruct(c.shape, F32) for c in caches]
        + [jax.ShapeDtypeStruct((nb, 2, 2 * ng, hw), F32)],
        input_output_aliases=aliases,
        scratch_shapes=[pltpu.VMEM((tm + POOL_STATE + 1, D_MODEL), F32)] + _skew_scratch(tm),
        compiler_params=_params("arbitrary"),
        name="pool_mlp",
    )(*args)
    return res[0].reshape(b, s, D_MODEL), res[1], list(res[2:2 + ng]), res[2 + ng]


def _pool_sample_kernel(x_ref, st_ref, g_ref, w_ref, sc_ref, o_ref, nst_ref, *, pos):
    x = x_ref[...]
    xn = _rmsnorm(x, g_ref[...])
    for gi, w in enumerate(POOL_WINDOWS):
        cols = slice(gi * POOL_GC, (gi + 1) * POOL_GC)
        cur = xn[:, cols]
        acc = cur
        for k in range(1, w):
            acc = acc + st_ref[POOL_STATE - k, :, cols]
        d = acc / float(min(pos + 1, w)) - cur
        y = jnp.dot(d.astype(BF16), w_ref[gi], preferred_element_type=F32)
        o_ref[:, cols] = x[:, cols] + y * sc_ref[:, cols]
    for k in range(POOL_STATE - 1):
        nst_ref[k] = st_ref[k + 1]
    nst_ref[POOL_STATE - 1] = xn


def _pool_sample(x, st, g, w, sc, pos):
    n = x.shape[0]
    return pl.pallas_call(
        functools.partial(_pool_sample_kernel, pos=pos),
        out_shape=[
            jax.ShapeDtypeStruct((n, D_MODEL), F32),
            jax.ShapeDtypeStruct((POOL_STATE, n, D_MODEL), F32),
        ],
        compiler_params=pltpu.CompilerParams(vmem_limit_bytes=VMEM_LIMIT),
        name="pool_sample",
    )(x, st, g, w, sc)


def _rope_tables(pos):
    half = HEAD_DIM // 2
    inv = ROPE_THETA ** (-np.arange(half, dtype=np.float64) * 2.0 / HEAD_DIM)
    ang = pos.astype(np.float64)[:, None] * inv[None, :]
    cos, sin = np.cos(ang), np.sin(ang)
    return (jnp.asarray(np.concatenate([cos, cos, cos, cos], axis=1), F32),
            jnp.asarray(np.concatenate([-sin, sin, -sin, sin], axis=1), F32))


def _gain_tables(gain):
    half = HEAD_DIM // 2
    swapped = jnp.concatenate([gain[:, half:], gain[:, :half]], axis=1)
    return jnp.concatenate([gain, gain], axis=1), jnp.concatenate([swapped, swapped], axis=1)


def _qkv_kernel(x_ref, g_ref, w_ref, gq_ref, gqs_ref, gk_ref, gks_ref, cos_ref, sin_ref, ones_ref,
                *rest, tm, dils, out_dtype, tail_lens, n_prev):
    rest = rest[n_prev:]
    outs = rest[:9]
    tails = rest[9:9 + len(tail_lens)]
    stage_ref = rest[9 + len(tail_lens)]
    kv_stage = rest[-1] if tail_lens else None
    x = x_ref[0]
    h = _rmsnorm(x, g_ref[...]).astype(BF16)
    cosf = cos_ref[...]
    sinf = sin_ref[...]
    lane = lax.broadcasted_iota(jnp.int32, (tm, LANES), 1)
    first_half = (lane % HEAD_DIM) < (HEAD_DIM // 2)
    units = [(part, gi, hh) for part in range(3) for gi in range(N_GROUPS)
             for hh in range(GROUP_W // MXU_DIM)]

    def project(part, gi, hh):
        c0 = (part * N_GROUPS + gi) * GROUP_W + hh * MXU_DIM
        return jnp.dot(h, w_ref[:, c0:c0 + MXU_DIM], preferred_element_type=F32)

    def rope_tables(part, gi):
        ta = (gq_ref, gk_ref)[part][gi:gi + 1, :] * cosf
        tb = (gqs_ref, gks_ref)[part][gi:gi + 1, :] * sinf
        return (ta * Q_SCALE, tb * Q_SCALE) if part == 0 else (ta, tb)

    ahead = [project(*u) for u in units[:QKV_LOOKAHEAD]]
    for idx, (part, gi, hh) in enumerate(units):
        res = ahead.pop(0)
        if idx + QKV_LOOKAHEAD < len(units):
            ahead.append(project(*units[idx + QKV_LOOKAHEAD]))
        slabs = [res[:, sl * LANES:(sl + 1) * LANES] for sl in range(MXU_DIM // LANES)]
        if part < 2:
            if hh == 0:
                ta, tb = rope_tables(part, gi)
            ss = jnp.dot((res * res).astype(BF16), ones_ref[...], preferred_element_type=F32)
            r = lax.rsqrt(ss * (1.0 / HEAD_DIM) + EPS)
            for sl, xs in enumerate(slabs):
                swapped = jnp.where(first_half, pltpu.roll(xs, LANES - HEAD_DIM // 2, 1),
                                    pltpu.roll(xs, HEAD_DIM // 2, 1))
                slabs[sl] = r[:, sl * LANES:(sl + 1) * LANES] * (xs * ta + swapped * tb)
        out_ref = outs[part * N_GROUPS + gi]
        d = dils[gi]
        for sl, val in enumerate(slabs):
            c = hh * (MXU_DIM // LANES) + sl
            lanes = slice(c * LANES, (c + 1) * LANES)
            if tail_lens and part > 0:
                t0 = (gi * 2 + part - 1) * GROUP_W
                kv_stage[:, t0 + c * LANES:t0 + (c + 1) * LANES] = val
            if d == 1:
                out_ref[0, 0, :, lanes] = val.astype(out_dtype)
            else:
                stage_ref[c] = val
                for r_ in range(d):
                    out_ref[0, r_, :, lanes] = stage_ref[c, pl.ds(r_, tm // d, stride=d), :].astype(out_dtype)

    j = pl.program_id(1)
    for gi, n in enumerate(tail_lens):
        rows = min(n, tm)

        @pl.when(j >= pl.num_programs(1) - max(n // tm, 1))
        def _(gi=gi, rows=rows):
            for part in range(2):
                t0 = (gi * 2 + part) * GROUP_W
                tails[gi][0, 0, part * GROUP_W:(part + 1) * GROUP_W, :] = (
                    kv_stage[tm - rows:tm, t0:t0 + GROUP_W].T)


def _qkv(x, g, w, li, gq, gk, cosf, sinf, ones, *, tm, dils, out_dtype, tail_lens=(), prev_tails=None):
    b, s, _ = x.shape
    nt = s // tm
    gq_t, gq_s = gq
    gk_t, gk_s = gk
    const2 = lambda i, j: (0, 0)
    in_specs = [
        pl.BlockSpec((1, tm, D_MODEL), lambda i, j: (i, j, 0)),
        pl.BlockSpec((1, D_MODEL), const2),
        pl.BlockSpec((None, D_MODEL, QKV_W), lambda i, j: (li, 0, 0)),
        pl.BlockSpec((N_GROUPS, LANES), const2),
        pl.BlockSpec((N_GROUPS, LANES), const2),
        pl.BlockSpec((N_GROUPS, LANES), const2),
        pl.BlockSpec((N_GROUPS, LANES), const2),
        pl.BlockSpec((tm, LANES), lambda i, j: (j, 0)),
        pl.BlockSpec((tm, LANES), lambda i, j: (j, 0)),
        pl.BlockSpec((MXU_DIM, MXU_DIM), const2),
    ]
    out_specs, out_shape = [], []
    for _ in range(3):
        for d in dils:
            out_specs.append(pl.BlockSpec((1, d, tm // d, GROUP_W), lambda i, j: (i, 0, j, 0)))
            out_shape.append(jax.ShapeDtypeStruct((b, d, s // d, GROUP_W), out_dtype))
    args = [x, g, w, gq_t, gq_s, gk_t, gk_s, cosf, sinf, ones]
    aliases = {}
    if prev_tails is not None:
        aliases = {len(args) + gi: 9 + gi for gi in range(len(tail_lens))}
        in_specs += [pl.BlockSpec(memory_space=pl.ANY)] * len(tail_lens)
        args += list(prev_tails)
    for n in tail_lens:
        first = nt - max(n // tm, 1)
        out_specs.append(pl.BlockSpec((1, 1, 2 * GROUP_W, min(n, tm)),
                                      lambda i, j, first=first: (li, i, 0, jnp.maximum(j - first, 0))))
        out_shape.append(jax.ShapeDtypeStruct((w.shape[0], b, 2 * GROUP_W, n), F32))
    scratch = [pltpu.VMEM((GROUP_W // LANES, tm, LANES), F32)]
    if tail_lens:
        scratch.append(pltpu.VMEM((tm, 2 * N_GROUPS * GROUP_W), F32))
    return pl.pallas_call(
        functools.partial(_qkv_kernel, tm=tm, dils=dils, out_dtype=out_dtype, tail_lens=tuple(tail_lens),
                          n_prev=len(args) - 10),
        grid=(b, nt),
        in_specs=in_specs,
        out_specs=out_specs,
        out_shape=out_shape,
        input_output_aliases=aliases,
        scratch_shapes=scratch,
        compiler_params=_params("arbitrary", "arbitrary"),
        name="qkv",
    )(*args)


def _stat_is_max(shape):
    lane = lax.broadcasted_iota(jnp.int32, shape, len(shape) - 1)
    return (lane % HEAD_DIM) < (HEAD_DIM // 2)


def _unpack_stats(packed):
    is_max = _stat_is_max(packed.shape)
    half = HEAD_DIM // 2
    m = jnp.where(is_max, packed, pltpu.roll(packed, half, 1))
    l = jnp.where(is_max, pltpu.roll(packed, LANES - half, 1), packed)
    return m, l


def _attn_kernel(q_ref, kp_ref, kc_ref, vp_ref, vc_ref, o_ref, ml_ref, kcat_ref, vcat_ref, p_scr, m_scr,
                 *, tq):
    j = pl.program_id(1)
    kcat_ref[0:Q_SUB, :] = kp_ref[0]
    kcat_ref[Q_SUB:Q_SUB + tq, :] = kc_ref[0]
    vcat_ref[0:Q_SUB, :] = vp_ref[0]
    vcat_ref[Q_SUB:Q_SUB + tq, :] = vc_ref[0]

    nk = 2 * Q_SUB
    nq = 2 * Q_SUB
    row = lax.broadcasted_iota(jnp.int32, (nq, nk), 0) & (Q_SUB - 1)
    col = lax.broadcasted_iota(jnp.int32, (nq, nk), 1)
    dist = row + Q_SUB - col
    band = (dist >= 0) & (dist <= N_KEYS_M1)
    lane = lax.broadcasted_iota(jnp.int32, (Q_SUB, LANES), 1)
    low_head = lane < HEAD_DIM
    stat_is_max = _stat_is_max((nq, LANES))
    zero_q = jnp.zeros((Q_SUB, LANES), BF16)
    ones_v = jnp.ones((nk, LANES), BF16)

    first_valid = band & ((col >= Q_SUB) | (j > 0))
    jobs = [(sb * Q_SUB, slice(p * LANES, (p + 1) * LANES))
            for sb in range(tq // Q_SUB) for p in range(GROUP_W // LANES)]

    def probabilities(idx):
        r0, lanes = jobs[idx]
        qp = q_ref[0, r0:r0 + Q_SUB, lanes]
        q2 = jnp.concatenate([jnp.where(low_head, qp, zero_q), jnp.where(low_head, zero_q, qp)], axis=0)
        s = lax.dot_general(q2, kcat_ref[r0:r0 + nk, lanes], (((1,), (1,)), ((), ())),
                            preferred_element_type=F32)
        s = jnp.where(first_valid if r0 == 0 else band, s, NEG)
        m = jnp.max(s, axis=1, keepdims=True)
        p_scr[idx % ATTN_STAGES] = jnp.exp2(s - m).astype(BF16)
        m_scr[idx % ATTN_STAGES] = jnp.broadcast_to(m, (nq, LANES))

    def outputs(idx):
        r0, lanes = jobs[idx]
        vaug = jnp.concatenate([vcat_ref[r0:r0 + nk, lanes], ones_v], axis=1)
        ov = jnp.dot(p_scr[idx % ATTN_STAGES], vaug, preferred_element_type=F32)
        num = ov[:, :LANES]
        ml = jnp.where(stat_is_max, m_scr[idx % ATTN_STAGES], ov[:, LANES:])
        o_ref[0, r0:r0 + Q_SUB, lanes] = jnp.where(low_head, num[:Q_SUB], num[Q_SUB:]).astype(BF16)
        ml_ref[0, r0:r0 + Q_SUB, lanes] = jnp.where(low_head, ml[:Q_SUB], ml[Q_SUB:])

    lag = ATTN_STAGES - 1
    for idx in range(len(jobs) + lag):
        if idx < len(jobs):
            probabilities(idx)
        if idx >= lag:
            outputs(idx - lag)


def _attn(q, k, v, tq):
    nseq, length, _ = q.shape
    ratio = tq // Q_SUB
    cur = lambda i, j: (i, j, 0)
    prev = lambda i, j: (i, jnp.maximum(j * ratio - 1, 0), 0)
    return pl.pallas_call(
        functools.partial(_attn_kernel, tq=tq),
        grid=(nseq, length // tq),
        in_specs=[
            pl.BlockSpec((1, tq, GROUP_W), cur),
            pl.BlockSpec((1, Q_SUB, GROUP_W), prev),
            pl.BlockSpec((1, tq, GROUP_W), cur),
            pl.BlockSpec((1, Q_SUB, GROUP_W), prev),
            pl.BlockSpec((1, tq, GROUP_W), cur),
        ],
        out_specs=[pl.BlockSpec((1, tq, GROUP_W), cur), pl.BlockSpec((1, tq, GROUP_W), cur)],
        out_shape=[jax.ShapeDtypeStruct((nseq, length, GROUP_W), BF16),
                   jax.ShapeDtypeStruct((nseq, length, GROUP_W), F32)],
        scratch_shapes=[pltpu.VMEM((tq + Q_SUB, GROUP_W), BF16)] * 2
        + [pltpu.VMEM((ATTN_STAGES, 2 * Q_SUB, 2 * Q_SUB), BF16),
           pltpu.VMEM((ATTN_STAGES, 2 * Q_SUB, LANES), F32)],
        compiler_params=_params("arbitrary", "arbitrary"),
        name="attn_prompt",
    )(q, k, k, v, v)


def _merge_pieces(load_x, store_y, o_refs, l_refs, wo_ref, und, tm, dils):
    nc = GROUP_W // LANES

    def slabs(ref, d, scr, r0, nr):
        if d == 1:
            return [ref[0, 0, r0:r0 + nr, c * LANES:(c + 1) * LANES] for c in range(nc)]
        for r_ in range(d):
            for c in range(nc):
                scr[c, pl.ds(r0 + r_, nr // d, stride=d), :] = (
                    ref[0, r_, r0 // d:(r0 + nr) // d, c * LANES:(c + 1) * LANES].astype(F32))
        return [scr[c, r0:r0 + nr, :] for c in range(nc)]

    def merge_rows(r0, nr):
        os_ = [slabs(o_refs[g], dils[g], und[2 * g], r0, nr) for g in range(N_GROUPS)]
        ss_ = [slabs(l_refs[g], dils[g], und[2 * g + 1], r0, nr) for g in range(N_GROUPS)]
        merged = []
        for c in range(nc):
            ms, ls_ = zip(*[_unpack_stats(ss_[g][c]) for g in range(N_GROUPS)])
            mx = jnp.maximum(jnp.maximum(ms[0], ms[1]), ms[2])
            ws = [jnp.exp2(m - mx) for m in ms]
            num = ws[0] * os_[0][c] + ws[1] * os_[1][c] + ws[2] * os_[2][c]
            den = ws[0] * ls_[0] + ws[1] * ls_[1] + ws[2] * ls_[2]
            merged.append((num / den).astype(BF16))
        merged = jnp.concatenate(merged, axis=1)
        store_y(r0, nr, load_x(r0, nr) + jnp.dot(merged, wo_ref[...], preferred_element_type=F32))

    nr = min(tm, Q_SUB)
    return [functools.partial(merge_rows, r0, nr) for r0 in range(0, tm, nr)]


def _merge_scratch(tm):
    return [pltpu.VMEM((GROUP_W // LANES, tm, LANES), F32)] * (2 * N_GROUPS)


def _merge_kernel(x_ref, o0, o1, o2, l0, l1, l2, wo_ref, y_ref, *scratch, tm, dils):
    def store_y(r0, nr, v):
        y_ref[0, r0:r0 + nr, :] = v

    for piece in _merge_pieces(lambda r0, nr: x_ref[0, r0:r0 + nr, :], store_y, (o0, o1, o2),
                               (l0, l1, l2), wo_ref, scratch, tm, dils):
        piece()


def _merge_mlp_kernel(x_ref, o0, o1, o2, l0, l1, l2, wo_ref, gmlp_ref, wu_ref, wd_ref, y_ref,
                      *scratch, tm, dils):
    und = scratch[:2 * N_GROUPS]
    mid_next, mid_cur, a_ref = scratch[2 * N_GROUPS:]

    @pl.when(pl.program_id(0) == 0)
    def _():
        mid_cur[...] = jnp.zeros((tm, D_MODEL), F32)

    def store_mid(r0, nr, v):
        mid_next[r0:r0 + nr, :] = v

    side = _merge_pieces(lambda r0, nr: x_ref[r0:r0 + nr, :], store_mid, (o0, o1, o2), (l0, l1, l2),
                         wo_ref, und, tm, dils)
    _mlp_residual(mid_cur[...], gmlp_ref[...], wu_ref, wd_ref, a_ref, y_ref, side, side_offset=1)
    mid_cur[...] = mid_next[...]


def _merge_mlp(x, os_, ls_, wo, li, gmlp, wu, wd, layer, *, tm, dils):
    b, s, _ = x.shape
    tps = s // tm
    n_tiles = b * tps
    x_spec, o_spec = _skew_specs(n_tiles, tm)

    def grp_map(i):
        t = jnp.minimum(i, n_tiles - 1)
        return (t // tps, 0, t % tps, 0)

    grp = [pl.BlockSpec((1, d, tm // d, GROUP_W), grp_map) for d in dils]
    y = pl.pallas_call(
        functools.partial(_merge_mlp_kernel, tm=tm, dils=dils),
        grid=(n_tiles + 1,),
        in_specs=[x_spec] + grp + grp
        + [pl.BlockSpec((None, GROUP_W, D_MODEL), lambda i: (li, 0, 0))] + _mlp_specs(layer),
        out_specs=o_spec,
        out_shape=jax.ShapeDtypeStruct((b * s, D_MODEL), F32),
        scratch_shapes=_merge_scratch(tm) + _skew_scratch(tm),
        compiler_params=_params("arbitrary"),
        name="merge_mlp",
    )(x.reshape(b * s, D_MODEL), *os_, *ls_, wo, gmlp, wu, wd)
    return y.reshape(b, s, D_MODEL)


def _merge(x, os_, ls_, wo, li, *, tm, dils):
    b, s, _ = x.shape
    grp = [pl.BlockSpec((1, d, tm // d, GROUP_W), lambda i, j: (i, 0, j, 0)) for d in dils]
    return pl.pallas_call(
        functools.partial(_merge_kernel, tm=tm, dils=dils),
        grid=(b, s // tm),
        in_specs=[pl.BlockSpec((1, tm, D_MODEL), lambda i, j: (i, j, 0))] + grp + grp
        + [pl.BlockSpec((None, GROUP_W, D_MODEL), lambda i, j: (li, 0, 0))],
        out_specs=pl.BlockSpec((1, tm, D_MODEL), lambda i, j: (i, j, 0)),
        out_shape=jax.ShapeDtypeStruct((b, s, D_MODEL), F32),
        scratch_shapes=_merge_scratch(tm),
        compiler_params=_params("arbitrary", "arbitrary"),
        name="merge_out_proj",
    )(x, *os_, *ls_, wo)


def _sample_attend_and_roll(kt, vt, q_row, kn_row, vn_row, new_k, new_v, n, dil):
    hw = kt.shape[0]
    nh = hw // HEAD_DIM
    sub = lax.broadcasted_iota(jnp.int32, (nh, hw), 0)
    ln = lax.broadcasted_iota(jnp.int32, (nh, hw), 1)
    own = (ln // HEAD_DIM) == sub
    e = lax.broadcasted_iota(jnp.int32, (nh, n), 1)
    qb = jnp.where(own, jnp.broadcast_to(q_row, (nh, hw)), 0.0)
    s = jnp.dot(qb.astype(BF16), kt.astype(BF16), preferred_element_type=F32)
    s = jnp.where((n - e) % dil == 0, s, NEG)
    s_new = jnp.sum(qb * kn_row, axis=1, keepdims=True)
    m = jnp.maximum(jnp.max(s, axis=1, keepdims=True), s_new)
    pr = jnp.exp2(s - m)
    p_new = jnp.exp2(s_new - m)
    l = jnp.sum(pr, axis=1, keepdims=True) + p_new
    o = lax.dot_general(pr.astype(BF16), vt.astype(BF16), (((1,), (1,)), ((), ())),
                        preferred_element_type=F32)
    o = o + p_new * vn_row
    o_row = jnp.sum(jnp.where(own, o, 0.0), axis=0, keepdims=True)
    stats = jnp.where(_stat_is_max((nh, hw)), m, l)
    stats_row = jnp.sum(jnp.where(own, stats, 0.0), axis=0, keepdims=True)
    return _shift_in(kt, new_k), _shift_in(vt, new_v), o_row, stats_row


def _shift_in(x, new_col):
    rows, n = x.shape
    lane0 = lax.broadcasted_iota(jnp.int32, (rows, LANES), 1) == 0
    out = []
    for j in range(n // LANES):
        cur = x[:, j * LANES:(j + 1) * LANES]
        nxt = x[:, (j + 1) * LANES:(j + 2) * LANES] if (j + 1) * LANES < n else new_col
        out.append(pltpu.roll(jnp.where(lane0, nxt, cur), LANES - 1, 1))
    return jnp.concatenate(out, axis=1)


def kernel(x_prompt, x_sample, state_pool, cache_kv_w128, cache_kv_w512, cache_kv_w2048, norm_mix,
           norm_mlp, pool_w, pool_scale, attn_w_qkv, attn_q_norm, attn_k_norm, attn_w_o, mlp_w_up,
           mlp_w_down):
    bsz, seq, _ = x_prompt.shape
    nb = x_sample.shape[0]
    past = PAST_LEN
    dils = tuple(d for _, d in ATTN_PATTERNS)
    ones_sample = (1, 1, 1)

    wu = mlp_w_up.astype(BF16)
    wd = mlp_w_down.astype(BF16)
    wqkv = attn_w_qkv.astype(BF16)
    wo = attn_w_o.astype(BF16)
    wp = pool_w.astype(BF16)
    head = lax.broadcasted_iota(jnp.int32, (MXU_DIM, MXU_DIM), 0) // HEAD_DIM
    ones = (head == head.T).astype(BF16)
    cos_p, sin_p = _rope_tables(np.arange(seq))
    cos_s, sin_s = _rope_tables(np.full((nb,), past))

    caches = [jnp.transpose(c, (0, 1, 3, 4, 5, 2)).reshape(c.shape[0], nb, 2, GROUP_W, c.shape[2])
              for c in (cache_kv_w128, cache_kv_w512, cache_kv_w2048)]

    xp = x_prompt
    xs = x_sample.reshape(nb, D_MODEL)
    pool_p, pool_s = [], []
    new_caches = kv_tails = None
    hw = GROUP_W // 2
    for layer in range(DEPTH):
        li = layer // 2
        g_mix = norm_mix[layer][None, :]
        g_mlp = norm_mlp[layer][None, :]
        gq = _gain_tables(attn_q_norm[li])
        gk = _gain_tables(attn_k_norm[li])
        if layer % 2 == 0:
            sc = pool_scale[li][None, :]
            st_in = jnp.transpose(state_pool[li], (1, 0, 2))
            xs, st = _pool_sample(xs, st_in, g_mix, wp[li], sc, past)
            pool_s.append(jnp.transpose(st, (1, 0, 2)))
            xs = _mlp(xs, g_mlp, wu, wd, layer, nb)
            outs = _qkv(xs[None], norm_mix[layer + 1][None, :], wqkv, li, gq, gk, cos_s, sin_s, ones,
                        tm=nb, dils=ones_sample, out_dtype=F32)
            rows = jnp.stack([o[0, 0] for o in outs[:9]], axis=1)
            qkn = jnp.transpose(rows.reshape(nb, 9, 2, hw), (0, 2, 1, 3))
            kvt = jnp.transpose(rows[:, N_GROUPS:].reshape(nb, 2, N_GROUPS, 2, hw), (3, 2, 1, 4, 0))
            xp, st, new_caches, sample_attn = _pool_mlp(xp, g_mix, wp[li], sc, g_mlp, wu, wd, layer,
                                                        POOL_TILE, caches, li, new_caches, qkn, kvt)
            pool_p.append(st)
        else:
            outs = _qkv(xp, g_mix, wqkv, li, gq, gk, cos_p, sin_p, ones, tm=ROW_TILE, dils=dils,
                        out_dtype=BF16, tail_lens=tuple(min(w, seq) for w, _ in ATTN_PATTERNS),
                        prev_tails=kv_tails)
            kv_tails = outs[9:]
            os_, ls_ = [], []
            for gi, (w, d) in enumerate(ATTN_PATTERNS):
                q, k, v = (outs[part * N_GROUPS + gi].reshape(bsz * d, seq // d, GROUP_W)
                           for part in range(3))
                o, stats = _attn(q, k, v, min(ATTN_TQ, seq // d))
                os_.append(o.reshape(bsz, d, seq // d, GROUP_W))
                ls_.append(stats.reshape(bsz, d, seq // d, GROUP_W))
            xp = _merge_mlp(xp, os_, ls_, wo, li, g_mlp, wu, wd, layer, tm=ROW_TILE, dils=dils)
            os_ = [sample_attn[:, :, gi, :].reshape(1, 1, nb, GROUP_W) for gi in range(N_GROUPS)]
            ls_ = [sample_attn[:, :, N_GROUPS + gi, :].reshape(1, 1, nb, GROUP_W) for gi in range(N_GROUPS)]
            xs = _merge(xs[None], os_, ls_, wo, li, tm=nb, dils=ones_sample)[0]
            xs = _mlp(xs, g_mlp, wu, wd, layer, nb)

    kv_s, kv_p = ([jnp.transpose(c.reshape(c.shape[0], c.shape[1], 2, HEADS, HEAD_DIM, c.shape[-1]),
                                 (0, 1, 5, 2, 3, 4)) for c in group] for group in (new_caches, kv_tails))
    return (xp, xs.reshape(nb, 1, D_MODEL), jnp.stack(pool_p), jnp.stack(pool_s),
            kv_p[0], kv_s[0], kv_p[1], kv_s[1], kv_p[2], kv_s[2])
```

```python
import functools

import jax
import jax.numpy as jnp
import numpy as np
from jax import lax
from jax.experimental import pallas as pl
from jax.experimental.pallas import tpu as pltpu

F32 = jnp.float32
BF16 = jnp.bfloat16

D_MODEL = 1024
D_FF = 4 * D_MODEL
DEPTH = 4
POOL_WINDOWS = (2, 4, 8, 16)
POOL_GC = D_MODEL // len(POOL_WINDOWS)
POOL_STATE = max(POOL_WINDOWS) - 1
ATTN_PATTERNS = ((128, 1), (512, 4), (2048, 16))
N_GROUPS = len(ATTN_PATTERNS)
HEAD_DIM = 64
HEADS = 8
GROUP_W = HEADS * HEAD_DIM
QKV_W = 3 * N_GROUPS * GROUP_W
N_KEYS_M1 = 128
ROPE_THETA = 10000.0
PAST_LEN = 8192
EPS = 1e-6
NEG = -1e30
LOG2E = 1.4426950408889634
Q_SCALE = HEAD_DIM ** -0.5 * LOG2E

LANES = 128
MXU_DIM = 256
VMEM_LIMIT = 56 * 1024 * 1024
ROW_TILE = 512
POOL_TILE = 256
FF_CHUNK = 512
DOWN_CHUNK = 256
ATTN_TQ = 1024
Q_SUB = 128
QKV_LOOKAHEAD = 1
ATTN_STAGES = 3


def _params(*sem):
    return pltpu.CompilerParams(dimension_semantics=sem, vmem_limit_bytes=VMEM_LIMIT)


def _rmsnorm(x, g):
    return x * lax.rsqrt(jnp.mean(x * x, axis=-1, keepdims=True) + EPS) * g


def _mlp_residual(x, g, wu_ref, wd_ref, a_ref, o_ref, side=(), side_offset=0):
    h = _rmsnorm(x, g).astype(BF16)
    n_up = D_FF // FF_CHUNK
    n_down = D_MODEL // DOWN_CHUNK
    slots = n_up + n_down - 1

    def run_side(slot):
        for j, piece in enumerate(side):
            if min((2 * j + side_offset) * slots // (2 * len(side)), slots - 1) == slot:
                piece()

    for c in range(n_up):
        cols = slice(c * FF_CHUNK, (c + 1) * FF_CHUNK)
        u = jnp.dot(h, wu_ref[:, cols], preferred_element_type=F32)
        a_ref[:, cols] = jnp.square(jnp.maximum(u, 0.0)).astype(BF16)
        run_side(c)
    for c in range(n_down):
        cols = slice(c * DOWN_CHUNK, (c + 1) * DOWN_CHUNK)
        o_ref[:, cols] = x[:, cols] + jnp.dot(a_ref[...], wd_ref[:, cols], preferred_element_type=F32)
        if c < n_down - 1:
            run_side(n_up + c)


def _mlp_kernel(x_ref, g_ref, wu_ref, wd_ref, o_ref, a_ref):
    _mlp_residual(x_ref[...], g_ref[...], wu_ref, wd_ref, a_ref, o_ref)


def _mlp(x, g, wu, wd, layer, tm):
    n = x.shape[0]
    return pl.pallas_call(
        _mlp_kernel,
        grid=(n // tm,),
        in_specs=[
            pl.BlockSpec((tm, D_MODEL), lambda i: (i, 0)),
            pl.BlockSpec((1, D_MODEL), lambda i: (0, 0)),
            pl.BlockSpec((None, D_MODEL, D_FF), lambda i: (layer, 0, 0)),
            pl.BlockSpec((None, D_FF, D_MODEL), lambda i: (layer, 0, 0)),
        ],
        out_specs=pl.BlockSpec((tm, D_MODEL), lambda i: (i, 0)),
        out_shape=jax.ShapeDtypeStruct((n, D_MODEL), F32),
        scratch_shapes=[pltpu.VMEM((tm, D_FF), BF16)],
        compiler_params=_params("arbitrary"),
        name="mlp",
    )(x, g, wu, wd)


def _skew_specs(n_tiles, tm):
    x_spec = pl.BlockSpec((tm, D_MODEL), lambda i: (jnp.minimum(i, n_tiles - 1), 0))
    o_spec = pl.BlockSpec((tm, D_MODEL), lambda i: (jnp.maximum(i - 1, 0), 0))
    return x_spec, o_spec


def _skew_scratch(tm):
    return [pltpu.VMEM((tm, D_MODEL), F32), pltpu.VMEM((tm, D_MODEL), F32), pltpu.VMEM((tm, D_FF), BF16)]


def _mlp_specs(layer):
    return [pl.BlockSpec((1, D_MODEL), lambda i: (0, 0)),
            pl.BlockSpec((None, D_MODEL, D_FF), lambda i: (layer, 0, 0)),
            pl.BlockSpec((None, D_FF, D_MODEL), lambda i: (layer, 0, 0))]


def _pool_mlp_kernel(x_ref, gmix_ref, w_ref, sc_ref, gmlp_ref, wu_ref, wd_ref, *rest,
                     tm, tiles_per_seq, n_tiles, buf_lens, aliased):
    ng = len(buf_lens)
    c_refs, (qkn_ref, kvt_ref) = rest[:ng], rest[ng:ng + 2]
    outs = rest[ng + 2 + (ng if aliased else 0):]
    o_ref, st_ref = outs[:2]
    co_refs, ol_ref = outs[2:2 + ng], outs[2 + ng]
    ext_ref, mid_next, mid_cur, a_ref = outs[3 + ng:]
    i = pl.program_id(0)
    halo = POOL_STATE + 1

    @pl.when(i == 0)
    def _():
        ext_ref[0:halo, :] = jnp.zeros((halo, D_MODEL), F32)
        mid_cur[...] = jnp.zeros((tm, D_MODEL), F32)

    s = jnp.minimum(i, n_tiles - 1) % tiles_per_seq
    ext_ref[0:halo, :] = jnp.where(s == 0, 0.0, ext_ref[0:halo, :])

    def normalise():
        ext_ref[halo:halo + tm, :] = _rmsnorm(x_ref[...], gmix_ref[...])

    def pool_block(gi, r0, nr):
        w = POOL_WINDOWS[gi]
        cols = slice(gi * POOL_GC, (gi + 1) * POOL_GC)
        pos = s * tm + r0 + lax.broadcasted_iota(jnp.int32, (nr, 1), 0)
        cur = ext_ref[halo + r0:halo + r0 + nr, cols]
        acc = cur
        for k in range(1, w):
            acc = acc + ext_ref[halo + r0 - k:halo + r0 - k + nr, cols]
        cnt = jnp.minimum(pos + 1, w).astype(F32)
        d = (acc / cnt - cur).astype(BF16)
        y = jnp.dot(d, w_ref[gi], preferred_element_type=F32)
        mid_next[r0:r0 + nr, cols] = x_ref[r0:r0 + nr, cols] + y * sc_ref[:, cols]

    t = jnp.minimum(i, n_tiles - 1)
    sample, half = t // 2, t % 2

    def buffer_group(g):
        kt, vt = c_refs[g][0, 0, 0], c_refs[g][0, 0, 1]
        lane = lax.broadcasted_iota(jnp.int32, kvt_ref.shape[3:], 1)
        new_k = jnp.sum(jnp.where(lane == sample, kvt_ref[half, g, 0], 0.0), axis=1, keepdims=True)
        new_v = jnp.sum(jnp.where(lane == sample, kvt_ref[half, g, 1], 0.0), axis=1, keepdims=True)
        rk, rv, o_row, lse_row = _sample_attend_and_roll(
            kt, vt, qkn_ref[0, 0, g:g + 1, :], qkn_ref[0, 0, ng + g:ng + g + 1, :],
            qkn_ref[0, 0, 2 * ng + g:2 * ng + g + 1, :], new_k, new_v, buf_lens[g], ATTN_PATTERNS[g][1])
        co_refs[g][0, 0, 0] = rk
        co_refs[g][0, 0, 1] = rv
        ol_ref[0, 0, g:g + 1, :] = o_row
        ol_ref[0, 0, ng + g:ng + g + 1, :] = lse_row

    side = [normalise]
    for gi, splits in ((3, 4), (2, 2), (1, 1), (0, 1)):
        nr = tm // splits
        side += [functools.partial(pool_block, gi, p * nr, nr) for p in range(splits)]
    for g in range(ng):
        side.insert(1 + 3 * g, functools.partial(buffer_group, ng - 1 - g))
    _mlp_residual(mid_cur[...], gmlp_ref[...], wu_ref, wd_ref, a_ref, o_ref, side)

    st_ref[0] = ext_ref[tm + 1:tm + halo, :]
    ext_ref[0:halo, :] = ext_ref[tm:tm + halo, :]
    mid_cur[...] = mid_next[...]


def _pool_mlp(x, gmix, w, sc, gmlp, wu, wd, layer, tm, caches, li, prev_rolled, qkn, kvt):
    b, s, _ = x.shape
    tps = s // tm
    n_tiles = b * tps
    nb = qkn.shape[0]
    hw = GROUP_W // 2
    assert n_tiles == 2 * nb, "one half of one sample's heads rides along with every row tile"
    ng = len(caches)
    buf_lens = tuple(c.shape[-1] for c in caches)
    aliased = prev_rolled is not None
    x_spec, o_spec = _skew_specs(n_tiles, tm)
    const = lambda i: (0, 0)

    def half_sample(i):
        t = jnp.minimum(i, n_tiles - 1)
        return t // 2, t % 2

    def cache_map(i):
        smp, half = half_sample(i)
        return (li, smp, 0, half, 0)

    def row_map(i):
        smp, half = half_sample(i)
        return (smp, half, 0, 0)

    cblk = [pl.BlockSpec((1, 1, 2, hw, n), cache_map) for n in buf_lens]
    in_specs = [
        x_spec,
        pl.BlockSpec((1, D_MODEL), const),
        pl.BlockSpec((len(POOL_WINDOWS), POOL_GC, POOL_GC), lambda i: (0, 0, 0)),
        pl.BlockSpec((1, D_MODEL), const),
    ] + _mlp_specs(layer) + cblk + [
        pl.BlockSpec((1, 1, 3 * ng, hw), row_map),
        pl.BlockSpec(kvt.shape, lambda i: (0, 0, 0, 0, 0)),
    ]
    args = [x.reshape(b * s, D_MODEL), gmix, w, sc, gmlp, wu, wd, *caches, qkn, kvt]
    aliases = {}
    if aliased:
        aliases = {len(args) + g: 2 + g for g in range(ng)}
        in_specs += [pl.BlockSpec(memory_space=pl.ANY)] * ng
        args += list(prev_rolled)
    res = pl.pallas_call(
        functools.partial(_pool_mlp_kernel, tm=tm, tiles_per_seq=tps, n_tiles=n_tiles,
                          buf_lens=buf_lens, aliased=aliased),
        grid=(n_tiles + 1,),
        in_specs=in_specs,
        out_specs=[
            o_spec,
            pl.BlockSpec((1, POOL_STATE, D_MODEL),
                         lambda i: (jnp.minimum(i, n_tiles - 1) // tps, 0, 0)),
        ] + cblk + [pl.BlockSpec((1, 1, 2 * ng, hw), row_map)],
        out_shape=[
            jax.ShapeDtypeStruct((b * s, D_MODEL), F32),
            jax.ShapeDtypeStruct((b, POOL_STATE, D_MODEL), F32),
        ] + [jax.ShapeDtypeStruct(c.shape, F32) for c in caches]
        + [jax.ShapeDtypeStruct((nb, 2, 2 * ng, hw), F32)],
        input_output_aliases=aliases,
        scratch_shapes=[pltpu.VMEM((tm + POOL_STATE + 1, D_MODEL), F32)] + _skew_scratch(tm),
        compiler_params=_params("arbitrary"),
        name="pool_mlp",
    )(*args)
    return res[0].reshape(b, s, D_MODEL), res[1], list(res[2:2 + ng]), res[2 + ng]


def _pool_sample_kernel(x_ref, st_ref, g_ref, w_ref, sc_ref, o_ref, nst_ref, *, pos):
    x = x_ref[...]
    xn = _rmsnorm(x, g_ref[...])
    for gi, w in enumerate(POOL_WINDOWS):
        cols = slice(gi * POOL_GC, (gi + 1) * POOL_GC)
        cur = xn[:, cols]
        acc = cur
        for k in range(1, w):
            acc = acc + st_ref[POOL_STATE - k, :, cols]
        d = acc / float(min(pos + 1, w)) - cur
        y = jnp.dot(d.astype(BF16), w_ref[gi], preferred_element_type=F32)
        o_ref[:, cols] = x[:, cols] + y * sc_ref[:, cols]
    for k in range(POOL_STATE - 1):
        nst_ref[k] = st_ref[k + 1]
    nst_ref[POOL_STATE - 1] = xn


def _pool_sample(x, st, g, w, sc, pos):
    n = x.shape[0]
    return pl.pallas_call(
        functools.partial(_pool_sample_kernel, pos=pos),
        out_shape=[
            jax.ShapeDtypeStruct((n, D_MODEL), F32),
            jax.ShapeDtypeStruct((POOL_STATE, n, D_MODEL), F32),
        ],
        compiler_params=pltpu.CompilerParams(vmem_limit_bytes=VMEM_LIMIT),
        name="pool_sample",
    )(x, st, g, w, sc)


def _rope_tables(pos):
    half = HEAD_DIM // 2
    inv = ROPE_THETA ** (-np.arange(half, dtype=np.float64) * 2.0 / HEAD_DIM)
    ang = pos.astype(np.float64)[:, None] * inv[None, :]
    cos, sin = np.cos(ang), np.sin(ang)
    return (jnp.asarray(np.concatenate([cos, cos, cos, cos], axis=1), F32),
            jnp.asarray(np.concatenate([-sin, sin, -sin, sin], axis=1), F32))


def _gain_tables(gain):
    half = HEAD_DIM // 2
    swapped = jnp.concatenate([gain[:, half:], gain[:, :half]], axis=1)
    return jnp.concatenate([gain, gain], axis=1), jnp.concatenate([swapped, swapped], axis=1)


def _qkv_kernel(x_ref, g_ref, w_ref, gq_ref, gqs_ref, gk_ref, gks_ref, cos_ref, sin_ref, ones_ref,
                *rest, tm, dils, out_dtype, tail_lens, n_prev):
    rest = rest[n_prev:]
    n_out = 3 * N_GROUPS if dils else 1
    outs = rest[:n_out]
    tails = rest[n_out:n_out + len(tail_lens)]
    stage_ref = rest[n_out + len(tail_lens)]
    kv_stage = rest[-1] if tail_lens else None
    x = x_ref[0]
    h = _rmsnorm(x, g_ref[...]).astype(BF16)
    cosf = cos_ref[...]
    sinf = sin_ref[...]
    lane = lax.broadcasted_iota(jnp.int32, (tm, LANES), 1)
    first_half = (lane % HEAD_DIM) < (HEAD_DIM // 2)
    units = [(part, gi, hh) for part in range(3) for gi in range(N_GROUPS)
             for hh in range(GROUP_W // MXU_DIM)]

    def project(part, gi, hh):
        c0 = (part * N_GROUPS + gi) * GROUP_W + hh * MXU_DIM
        return jnp.dot(h, w_ref[:, c0:c0 + MXU_DIM], preferred_element_type=F32)

    def rope_tables(part, gi):
        ta = (gq_ref, gk_ref)[part][gi:gi + 1, :] * cosf
        tb = (gqs_ref, gks_ref)[part][gi:gi + 1, :] * sinf
        return (ta * Q_SCALE, tb * Q_SCALE) if part == 0 else (ta, tb)

    ahead = [project(*u) for u in units[:QKV_LOOKAHEAD]]
    for idx, (part, gi, hh) in enumerate(units):
        res = ahead.pop(0)
        if idx + QKV_LOOKAHEAD < len(units):
            ahead.append(project(*units[idx + QKV_LOOKAHEAD]))
        slabs = [res[:, sl * LANES:(sl + 1) * LANES] for sl in range(MXU_DIM // LANES)]
        if part < 2:
            if hh == 0:
                ta, tb = rope_tables(part, gi)
            ss = jnp.dot((res * res).astype(BF16), ones_ref[...], preferred_element_type=F32)
            r = lax.rsqrt(ss * (1.0 / HEAD_DIM) + EPS)
            for sl, xs in enumerate(slabs):
                swapped = jnp.where(first_half, pltpu.roll(xs, LANES - HEAD_DIM // 2, 1),
                                    pltpu.roll(xs, HEAD_DIM // 2, 1))
                slabs[sl] = r[:, sl * LANES:(sl + 1) * LANES] * (xs * ta + swapped * tb)
        out_ref = outs[part * N_GROUPS + gi] if dils else None
        d = dils[gi] if dils else 0
        for sl, val in enumerate(slabs):
            c = hh * (MXU_DIM // LANES) + sl
            lanes = slice(c * LANES, (c + 1) * LANES)
            if tail_lens and part > 0:
                t0 = (gi * 2 + part - 1) * GROUP_W
                kv_stage[:, t0 + c * LANES:t0 + (c + 1) * LANES] = val
            if not dils:
                outs[0][part * N_GROUPS + gi, :, lanes] = val.astype(out_dtype)
            elif d == 1:
                out_ref[0, 0, :, lanes] = val.astype(out_dtype)
            else:
                stage_ref[c] = val
                for r_ in range(d):
                    out_ref[0, r_, :, lanes] = stage_ref[c, pl.ds(r_, tm // d, stride=d), :].astype(out_dtype)

    j = pl.program_id(1)
    for gi, n in enumerate(tail_lens):
        rows = min(n, tm)

        @pl.when(j >= pl.num_programs(1) - max(n // tm, 1))
        def _(gi=gi, rows=rows):
            for part in range(2):
                t0 = (gi * 2 + part) * GROUP_W
                tails[gi][0, 0, part * GROUP_W:(part + 1) * GROUP_W, :] = (
                    kv_stage[tm - rows:tm, t0:t0 + GROUP_W].T)


def _qkv(x, g, w, li, gq, gk, cosf, sinf, ones, *, tm, dils, out_dtype, tail_lens=(), prev_tails=None):
    b, s, _ = x.shape
    nt = s // tm
    gq_t, gq_s = gq
    gk_t, gk_s = gk
    const2 = lambda i, j: (0, 0)
    in_specs = [
        pl.BlockSpec((1, tm, D_MODEL), lambda i, j: (i, j, 0)),
        pl.BlockSpec((1, D_MODEL), const2),
        pl.BlockSpec((None, D_MODEL, QKV_W), lambda i, j: (li, 0, 0)),
        pl.BlockSpec((N_GROUPS, LANES), const2),
        pl.BlockSpec((N_GROUPS, LANES), const2),
        pl.BlockSpec((N_GROUPS, LANES), const2),
        pl.BlockSpec((N_GROUPS, LANES), const2),
        pl.BlockSpec((tm, LANES), lambda i, j: (j, 0)),
        pl.BlockSpec((tm, LANES), lambda i, j: (j, 0)),
        pl.BlockSpec((MXU_DIM, MXU_DIM), const2),
    ]
    out_specs, out_shape = [], []
    for _ in range(3):
        for d in dils:
            out_specs.append(pl.BlockSpec((1, d, tm // d, GROUP_W), lambda i, j: (i, 0, j, 0)))
            out_shape.append(jax.ShapeDtypeStruct((b, d, s // d, GROUP_W), out_dtype))
    if not dils:
        assert b == 1 and not tail_lens
        out_specs.append(pl.BlockSpec((3 * N_GROUPS, tm, GROUP_W), lambda i, j: (0, j, 0)))
        out_shape.append(jax.ShapeDtypeStruct((3 * N_GROUPS, s, GROUP_W), out_dtype))
    args = [x, g, w, gq_t, gq_s, gk_t, gk_s, cosf, sinf, ones]
    aliases = {}
    if prev_tails is not None:
        aliases = {len(args) + gi: 9 + gi for gi in range(len(tail_lens))}
        in_specs += [pl.BlockSpec(memory_space=pl.ANY)] * len(tail_lens)
        args += list(prev_tails)
    for n in tail_lens:
        first = nt - max(n // tm, 1)
        out_specs.append(pl.BlockSpec((1, 1, 2 * GROUP_W, min(n, tm)),
                                      lambda i, j, first=first: (li, i, 0, jnp.maximum(j - first, 0))))
        out_shape.append(jax.ShapeDtypeStruct((w.shape[0], b, 2 * GROUP_W, n), F32))
    scratch = [pltpu.VMEM((GROUP_W // LANES, tm, LANES), F32)]
    if tail_lens:
        scratch.append(pltpu.VMEM((tm, 2 * N_GROUPS * GROUP_W), F32))
    return pl.pallas_call(
        functools.partial(_qkv_kernel, tm=tm, dils=dils, out_dtype=out_dtype, tail_lens=tuple(tail_lens),
                          n_prev=len(args) - 10),
        grid=(b, nt),
        in_specs=in_specs,
        out_specs=out_specs,
        out_shape=out_shape,
        input_output_aliases=aliases,
        scratch_shapes=scratch,
        compiler_params=_params("arbitrary", "arbitrary"),
        name="qkv",
    )(*args)


def _stat_is_max(shape):
    lane = lax.broadcasted_iota(jnp.int32, shape, len(shape) - 1)
    return (lane % HEAD_DIM) < (HEAD_DIM // 2)


def _unpack_stats(packed):
    is_max = _stat_is_max(packed.shape)
    half = HEAD_DIM // 2
    m = jnp.where(is_max, packed, pltpu.roll(packed, half, 1))
    l = jnp.where(is_max, pltpu.roll(packed, LANES - half, 1), packed)
    return m, l


def _attn_kernel(q_ref, kp_ref, kc_ref, vp_ref, vc_ref, o_ref, ml_ref, kcat_ref, vcat_ref, p_scr, m_scr,
                 *, tq):
    j = pl.program_id(1)
    kcat_ref[0:Q_SUB, :] = kp_ref[0]
    kcat_ref[Q_SUB:Q_SUB + tq, :] = kc_ref[0]
    vcat_ref[0:Q_SUB, :] = vp_ref[0]
    vcat_ref[Q_SUB:Q_SUB + tq, :] = vc_ref[0]

    nk = 2 * Q_SUB
    nq = 2 * Q_SUB
    row = lax.broadcasted_iota(jnp.int32, (nq, nk), 0) & (Q_SUB - 1)
    col = lax.broadcasted_iota(jnp.int32, (nq, nk), 1)
    dist = row + Q_SUB - col
    band = (dist >= 0) & (dist <= N_KEYS_M1)
    lane = lax.broadcasted_iota(jnp.int32, (Q_SUB, LANES), 1)
    low_head = lane < HEAD_DIM
    stat_is_max = _stat_is_max((nq, LANES))
    zero_q = jnp.zeros((Q_SUB, LANES), BF16)
    ones_v = jnp.ones((nk, LANES), BF16)

    first_valid = band & ((col >= Q_SUB) | (j > 0))
    jobs = [(sb * Q_SUB, slice(p * LANES, (p + 1) * LANES))
            for sb in range(tq // Q_SUB) for p in range(GROUP_W // LANES)]

    def probabilities(idx):
        r0, lanes = jobs[idx]
        qp = q_ref[0, r0:r0 + Q_SUB, lanes]
        q2 = jnp.concatenate([jnp.where(low_head, qp, zero_q), jnp.where(low_head, zero_q, qp)], axis=0)
        s = lax.dot_general(q2, kcat_ref[r0:r0 + nk, lanes], (((1,), (1,)), ((), ())),
                            preferred_element_type=F32)
        s = jnp.where(first_valid if r0 == 0 else band, s, NEG)
        m = jnp.max(s, axis=1, keepdims=True)
        p_scr[idx % ATTN_STAGES] = jnp.exp2(s - m).astype(BF16)
        m_scr[idx % ATTN_STAGES] = jnp.broadcast_to(m, (nq, LANES))

    def outputs(idx):
        r0, lanes = jobs[idx]
        vaug = jnp.concatenate([vcat_ref[r0:r0 + nk, lanes], ones_v], axis=1)
        ov = jnp.dot(p_scr[idx % ATTN_STAGES], vaug, preferred_element_type=F32)
        num = ov[:, :LANES]
        ml = jnp.where(stat_is_max, m_scr[idx % ATTN_STAGES], ov[:, LANES:])
        o_ref[0, r0:r0 + Q_SUB, lanes] = jnp.where(low_head, num[:Q_SUB], num[Q_SUB:]).astype(BF16)
        ml_ref[0, r0:r0 + Q_SUB, lanes] = jnp.where(low_head, ml[:Q_SUB], ml[Q_SUB:])

    lag = ATTN_STAGES - 1
    for idx in range(len(jobs) + lag):
        if idx < len(jobs):
            probabilities(idx)
        if idx >= lag:
            outputs(idx - lag)


def _attn(q, k, v, tq):
    nseq, length, _ = q.shape
    ratio = tq // Q_SUB
    cur = lambda i, j: (i, j, 0)
    prev = lambda i, j: (i, jnp.maximum(j * ratio - 1, 0), 0)
    return pl.pallas_call(
        functools.partial(_attn_kernel, tq=tq),
        grid=(nseq, length // tq),
        in_specs=[
            pl.BlockSpec((1, tq, GROUP_W), cur),
            pl.BlockSpec((1, Q_SUB, GROUP_W), prev),
            pl.BlockSpec((1, tq, GROUP_W), cur),
            pl.BlockSpec((1, Q_SUB, GROUP_W), prev),
            pl.BlockSpec((1, tq, GROUP_W), cur),
        ],
        out_specs=[pl.BlockSpec((1, tq, GROUP_W), cur), pl.BlockSpec((1, tq, GROUP_W), cur)],
        out_shape=[jax.ShapeDtypeStruct((nseq, length, GROUP_W), BF16),
                   jax.ShapeDtypeStruct((nseq, length, GROUP_W), F32)],
        scratch_shapes=[pltpu.VMEM((tq + Q_SUB, GROUP_W), BF16)] * 2
        + [pltpu.VMEM((ATTN_STAGES, 2 * Q_SUB, 2 * Q_SUB), BF16),
           pltpu.VMEM((ATTN_STAGES, 2 * Q_SUB, LANES), F32)],
        compiler_params=_params("arbitrary", "arbitrary"),
        name="attn_prompt",
    )(q, k, k, v, v)


def _merge_pieces(load_x, store_y, o_refs, l_refs, wo_ref, und, tm, dils):
    nc = GROUP_W // LANES

    def slabs(ref, d, scr, r0, nr):
        if d == 1:
            return [ref[0, 0, r0:r0 + nr, c * LANES:(c + 1) * LANES] for c in range(nc)]
        for r_ in range(d):
            for c in range(nc):
                scr[c, pl.ds(r0 + r_, nr // d, stride=d), :] = (
                    ref[0, r_, r0 // d:(r0 + nr) // d, c * LANES:(c + 1) * LANES].astype(F32))
        return [scr[c, r0:r0 + nr, :] for c in range(nc)]

    def merge_rows(r0, nr):
        os_ = [slabs(o_refs[g], dils[g], und[2 * g], r0, nr) for g in range(N_GROUPS)]
        ss_ = [slabs(l_refs[g], dils[g], und[2 * g + 1], r0, nr) for g in range(N_GROUPS)]
        merged = []
        for c in range(nc):
            ms, ls_ = zip(*[_unpack_stats(ss_[g][c]) for g in range(N_GROUPS)])
            mx = jnp.maximum(jnp.maximum(ms[0], ms[1]), ms[2])
            ws = [jnp.exp2(m - mx) for m in ms]
            num = ws[0] * os_[0][c] + ws[1] * os_[1][c] + ws[2] * os_[2][c]
            den = ws[0] * ls_[0] + ws[1] * ls_[1] + ws[2] * ls_[2]
            merged.append((num / den).astype(BF16))
        merged = jnp.concatenate(merged, axis=1)
        store_y(r0, nr, load_x(r0, nr) + jnp.dot(merged, wo_ref[...], preferred_element_type=F32))

    nr = min(tm, Q_SUB)
    return [functools.partial(merge_rows, r0, nr) for r0 in range(0, tm, nr)]


def _merge_scratch(tm):
    return [pltpu.VMEM((GROUP_W // LANES, tm, LANES), F32)] * (2 * N_GROUPS)


def _merge_kernel(x_ref, o0, o1, o2, l0, l1, l2, wo_ref, y_ref, *scratch, tm, dils):
    def store_y(r0, nr, v):
        y_ref[0, r0:r0 + nr, :] = v

    for piece in _merge_pieces(lambda r0, nr: x_ref[0, r0:r0 + nr, :], store_y, (o0, o1, o2),
                               (l0, l1, l2), wo_ref, scratch, tm, dils):
        piece()


def _merge_mlp_kernel(x_ref, o0, o1, o2, l0, l1, l2, wo_ref, gmlp_ref, wu_ref, wd_ref, y_ref,
                      *scratch, tm, dils):
    und = scratch[:2 * N_GROUPS]
    mid_next, mid_cur, a_ref = scratch[2 * N_GROUPS:]

    @pl.when(pl.program_id(0) == 0)
    def _():
        mid_cur[...] = jnp.zeros((tm, D_MODEL), F32)

    def store_mid(r0, nr, v):
        mid_next[r0:r0 + nr, :] = v

    side = _merge_pieces(lambda r0, nr: x_ref[r0:r0 + nr, :], store_mid, (o0, o1, o2), (l0, l1, l2),
                         wo_ref, und, tm, dils)
    _mlp_residual(mid_cur[...], gmlp_ref[...], wu_ref, wd_ref, a_ref, y_ref, side, side_offset=1)
    mid_cur[...] = mid_next[...]


def _merge_mlp(x, os_, ls_, wo, li, gmlp, wu, wd, layer, *, tm, dils):
    b, s, _ = x.shape
    tps = s // tm
    n_tiles = b * tps
    x_spec, o_spec = _skew_specs(n_tiles, tm)

    def grp_map(i):
        t = jnp.minimum(i, n_tiles - 1)
        return (t // tps, 0, t % tps, 0)

    grp = [pl.BlockSpec((1, d, tm // d, GROUP_W), grp_map) for d in dils]
    y = pl.pallas_call(
        functools.partial(_merge_mlp_kernel, tm=tm, dils=dils),
        grid=(n_tiles + 1,),
        in_specs=[x_spec] + grp + grp
        + [pl.BlockSpec((None, GROUP_W, D_MODEL), lambda i: (li, 0, 0))] + _mlp_specs(layer),
        out_specs=o_spec,
        out_shape=jax.ShapeDtypeStruct((b * s, D_MODEL), F32),
        scratch_shapes=_merge_scratch(tm) + _skew_scratch(tm),
        compiler_params=_params("arbitrary"),
        name="merge_mlp",
    )(x.reshape(b * s, D_MODEL), *os_, *ls_, wo, gmlp, wu, wd)
    return y.reshape(b, s, D_MODEL)


def _merge(x, attn, wo, li, *, tm):
    b, s, _ = x.shape
    dils = (1,) * N_GROUPS
    grp = [pl.BlockSpec((None, 1, 1, tm, GROUP_W), lambda i, j, k=k: (k, i, 0, j, 0))
           for k in range(2 * N_GROUPS)]
    return pl.pallas_call(
        functools.partial(_merge_kernel, tm=tm, dils=dils),
        grid=(b, s // tm),
        in_specs=[pl.BlockSpec((1, tm, D_MODEL), lambda i, j: (i, j, 0))] + grp
        + [pl.BlockSpec((None, GROUP_W, D_MODEL), lambda i, j: (li, 0, 0))],
        out_specs=pl.BlockSpec((1, tm, D_MODEL), lambda i, j: (i, j, 0)),
        out_shape=jax.ShapeDtypeStruct((b, s, D_MODEL), F32),
        scratch_shapes=_merge_scratch(tm),
        compiler_params=_params("arbitrary", "arbitrary"),
        name="merge_out_proj",
    )(x, *([attn] * (2 * N_GROUPS)), wo)


def _sample_attend_and_roll(kt, vt, q_row, kn_row, vn_row, new_k, new_v, n, dil):
    hw = kt.shape[0]
    nh = hw // HEAD_DIM
    sub = lax.broadcasted_iota(jnp.int32, (nh, hw), 0)
    ln = lax.broadcasted_iota(jnp.int32, (nh, hw), 1)
    own = (ln // HEAD_DIM) == sub
    e = lax.broadcasted_iota(jnp.int32, (nh, n), 1)
    qb = jnp.where(own, jnp.broadcast_to(q_row, (nh, hw)), 0.0)
    s = jnp.dot(qb.astype(BF16), kt.astype(BF16), preferred_element_type=F32)
    s = jnp.where((n - e) % dil == 0, s, NEG)
    s_new = jnp.sum(qb * kn_row, axis=1, keepdims=True)
    m = jnp.maximum(jnp.max(s, axis=1, keepdims=True), s_new)
    pr = jnp.exp2(s - m)
    p_new = jnp.exp2(s_new - m)
    l = jnp.sum(pr, axis=1, keepdims=True) + p_new
    o = lax.dot_general(pr.astype(BF16), vt.astype(BF16), (((1,), (1,)), ((), ())),
                        preferred_element_type=F32)
    o = o + p_new * vn_row
    o_row = jnp.sum(jnp.where(own, o, 0.0), axis=0, keepdims=True)
    stats = jnp.where(_stat_is_max((nh, hw)), m, l)
    stats_row = jnp.sum(jnp.where(own, stats, 0.0), axis=0, keepdims=True)
    return _shift_in(kt, new_k), _shift_in(vt, new_v), o_row, stats_row


def _shift_in(x, new_col):
    rows, n = x.shape
    lane0 = lax.broadcasted_iota(jnp.int32, (rows, LANES), 1) == 0
    out = []
    for j in range(n // LANES):
        cur = x[:, j * LANES:(j + 1) * LANES]
        nxt = x[:, (j + 1) * LANES:(j + 2) * LANES] if (j + 1) * LANES < n else new_col
        out.append(pltpu.roll(jnp.where(lane0, nxt, cur), LANES - 1, 1))
    return jnp.concatenate(out, axis=1)


def kernel(x_prompt, x_sample, state_pool, cache_kv_w128, cache_kv_w512, cache_kv_w2048, norm_mix,
           norm_mlp, pool_w, pool_scale, attn_w_qkv, attn_q_norm, attn_k_norm, attn_w_o, mlp_w_up,
           mlp_w_down):
    bsz, seq, _ = x_prompt.shape
    nb = x_sample.shape[0]
    past = PAST_LEN
    dils = tuple(d for _, d in ATTN_PATTERNS)

    wu = mlp_w_up.astype(BF16)
    wd = mlp_w_down.astype(BF16)
    wqkv = attn_w_qkv.astype(BF16)
    wo = attn_w_o.astype(BF16)
    wp = pool_w.astype(BF16)
    head = lax.broadcasted_iota(jnp.int32, (MXU_DIM, MXU_DIM), 0) // HEAD_DIM
    ones = (head == head.T).astype(BF16)
    cos_p, sin_p = _rope_tables(np.arange(seq))
    cos_s, sin_s = _rope_tables(np.full((nb,), past))

    caches = [jnp.transpose(c, (0, 1, 3, 4, 5, 2)).reshape(c.shape[0], nb, 2, GROUP_W, c.shape[2])
              for c in (cache_kv_w128, cache_kv_w512, cache_kv_w2048)]

    xp = x_prompt
    xs = x_sample.reshape(nb, D_MODEL)
    pool_p, pool_s = [], []
    new_caches = kv_tails = None
    hw = GROUP_W // 2
    for layer in range(DEPTH):
        li = layer // 2
        g_mix = norm_mix[layer][None, :]
        g_mlp = norm_mlp[layer][None, :]
        gq = _gain_tables(attn_q_norm[li])
        gk = _gain_tables(attn_k_norm[li])
        if layer % 2 == 0:
            sc = pool_scale[li][None, :]
            st_in = jnp.transpose(state_pool[li], (1, 0, 2))
            xs, st = _pool_sample(xs, st_in, g_mix, wp[li], sc, past)
            pool_s.append(jnp.transpose(st, (1, 0, 2)))
            xs = _mlp(xs, g_mlp, wu, wd, layer, nb)
            packed = _qkv(xs[None], norm_mix[layer + 1][None, :], wqkv, li, gq, gk, cos_s, sin_s, ones,
                          tm=nb, dils=(), out_dtype=F32)[0]
            qkn = jnp.transpose(packed.reshape(9, nb, 2, hw), (1, 2, 0, 3))
            kvt = jnp.transpose(packed[N_GROUPS:].reshape(2, N_GROUPS, nb, 2, hw), (3, 1, 0, 4, 2))
            xp, st, new_caches, sample_attn = _pool_mlp(xp, g_mix, wp[li], sc, g_mlp, wu, wd, layer,
                                                        POOL_TILE, caches, li, new_caches, qkn, kvt)
            pool_p.append(st)
        else:
            outs = _qkv(xp, g_mix, wqkv, li, gq, gk, cos_p, sin_p, ones, tm=ROW_TILE, dils=dils,
                        out_dtype=BF16, tail_lens=tuple(min(w, seq) for w, _ in ATTN_PATTERNS),
                        prev_tails=kv_tails)
            kv_tails = outs[9:]
            os_, ls_ = [], []
            for gi, (w, d) in enumerate(ATTN_PATTERNS):
                q, k, v = (outs[part * N_GROUPS + gi].reshape(bsz * d, seq // d, GROUP_W)
                           for part in range(3))
                o, stats = _attn(q, k, v, min(ATTN_TQ, seq // d))
                os_.append(o.reshape(bsz, d, seq // d, GROUP_W))
                ls_.append(stats.reshape(bsz, d, seq // d, GROUP_W))
            xp = _merge_mlp(xp, os_, ls_, wo, li, g_mlp, wu, wd, layer, tm=ROW_TILE, dils=dils)
            attn = jnp.transpose(sample_attn, (2, 0, 1, 3)).reshape(2 * N_GROUPS, 1, 1, nb, GROUP_W)
            xs = _merge(xs[None], attn, wo, li, tm=nb)[0]
            xs = _mlp(xs, g_mlp, wu, wd, layer, nb)

    kv_s, kv_p = ([jnp.transpose(c.reshape(c.shape[0], c.shape[1], 2, HEADS, HEAD_DIM, c.shape[-1]),
                                 (0, 1, 5, 2, 3, 4)) for c in group] for group in (new_caches, kv_tails))
    return (xp, xs.reshape(nb, 1, D_MODEL), jnp.stack(pool_p), jnp.stack(pool_s),
            kv_p[0], kv_s[0], kv_p[1], kv_s[1], kv_p[2], kv_s[2])
```

```python
import functools

import jax
import jax.numpy as jnp
import numpy as np
from jax import lax
from jax.experimental import pallas as pl
from jax.experimental.pallas import tpu as pltpu

F32 = jnp.float32
BF16 = jnp.bfloat16

D_MODEL = 1024
D_FF = 4 * D_MODEL
DEPTH = 4
POOL_WINDOWS = (2, 4, 8, 16)
POOL_GC = D_MODEL // len(POOL_WINDOWS)
POOL_STATE = max(POOL_WINDOWS) - 1
ATTN_PATTERNS = ((128, 1), (512, 4), (2048, 16))
N_GROUPS = len(ATTN_PATTERNS)
HEAD_DIM = 64
HEADS = 8
GROUP_W = HEADS * HEAD_DIM
QKV_W = 3 * N_GROUPS * GROUP_W
N_KEYS_M1 = 128
ROPE_THETA = 10000.0
PAST_LEN = 8192
EPS = 1e-6
NEG = -1e30
LOG2E = 1.4426950408889634
Q_SCALE = HEAD_DIM ** -0.5 * LOG2E

LANES = 128
MXU_DIM = 256
VMEM_LIMIT = 56 * 1024 * 1024
ROW_TILE = 512
POOL_TILE = 256
FF_CHUNK = 512
DOWN_CHUNK = 256
ATTN_TQ = 1024
Q_SUB = 128
QKV_LOOKAHEAD = 1
ATTN_STAGES = 3


def _params(*sem):
    return pltpu.CompilerParams(dimension_semantics=sem, vmem_limit_bytes=VMEM_LIMIT)


def _rmsnorm(x, g):
    return x * lax.rsqrt(jnp.mean(x * x, axis=-1, keepdims=True) + EPS) * g


def _mlp_residual(x, g, wu_ref, wd_ref, a_ref, o_ref, side=(), side_offset=0):
    h = _rmsnorm(x, g).astype(BF16)
    n_up = D_FF // FF_CHUNK
    n_down = D_MODEL // DOWN_CHUNK
    slots = n_up + n_down - 1

    def run_side(slot):
        for j, piece in enumerate(side):
            if min((2 * j + side_offset) * slots // (2 * len(side)), slots - 1) == slot:
                piece()

    for c in range(n_up):
        cols = slice(c * FF_CHUNK, (c + 1) * FF_CHUNK)
        u = jnp.dot(h, wu_ref[:, cols], preferred_element_type=F32)
        a_ref[:, cols] = jnp.square(jnp.maximum(u, 0.0)).astype(BF16)
        run_side(c)
    for c in range(n_down):
        cols = slice(c * DOWN_CHUNK, (c + 1) * DOWN_CHUNK)
        o_ref[:, cols] = x[:, cols] + jnp.dot(a_ref[...], wd_ref[:, cols], preferred_element_type=F32)
        if c < n_down - 1:
            run_side(n_up + c)


def _mlp_kernel(x_ref, g_ref, wu_ref, wd_ref, o_ref, a_ref):
    _mlp_residual(x_ref[...], g_ref[...], wu_ref, wd_ref, a_ref, o_ref)


def _mlp(x, g, wu, wd, layer, tm):
    n = x.shape[0]
    return pl.pallas_call(
        _mlp_kernel,
        grid=(n // tm,),
        in_specs=[
            pl.BlockSpec((tm, D_MODEL), lambda i: (i, 0)),
            pl.BlockSpec((1, D_MODEL), lambda i: (0, 0)),
            pl.BlockSpec((D_MODEL, D_FF), lambda i: (0, 0)),
            pl.BlockSpec((D_FF, D_MODEL), lambda i: (0, 0)),
        ],
        out_specs=pl.BlockSpec((tm, D_MODEL), lambda i: (i, 0)),
        out_shape=jax.ShapeDtypeStruct((n, D_MODEL), F32),
        scratch_shapes=[pltpu.VMEM((tm, D_FF), BF16)],
        compiler_params=_params("arbitrary"),
        name="mlp",
    )(x, g, wu[layer], wd[layer])


def _skew_specs(n_tiles, tm):
    x_spec = pl.BlockSpec((tm, D_MODEL), lambda i: (jnp.minimum(i, n_tiles - 1), 0))
    o_spec = pl.BlockSpec((tm, D_MODEL), lambda i: (jnp.maximum(i - 1, 0), 0))
    return x_spec, o_spec


def _skew_scratch(tm):
    return [pltpu.VMEM((tm, D_MODEL), F32), pltpu.VMEM((tm, D_MODEL), F32), pltpu.VMEM((tm, D_FF), BF16)]


def _mlp_specs():
    return [pl.BlockSpec((1, D_MODEL), lambda i: (0, 0)),
            pl.BlockSpec((D_MODEL, D_FF), lambda i: (0, 0)),
            pl.BlockSpec((D_FF, D_MODEL), lambda i: (0, 0))]


def _pool_mlp_kernel(x_ref, gmix_ref, w_ref, sc_ref, gmlp_ref, wu_ref, wd_ref, *rest,
                     tm, tiles_per_seq, n_tiles, buf_lens, aliased):
    ng = len(buf_lens)
    c_refs, (qkn_ref, kvt_ref) = rest[:ng], rest[ng:ng + 2]
    outs = rest[ng + 2 + (ng if aliased else 0):]
    o_ref, st_ref = outs[:2]
    co_refs, ol_ref = outs[2:2 + ng], outs[2 + ng]
    ext_ref, mid_next, mid_cur, a_ref = outs[3 + ng:]
    i = pl.program_id(0)
    halo = POOL_STATE + 1

    @pl.when(i == 0)
    def _():
        ext_ref[0:halo, :] = jnp.zeros((halo, D_MODEL), F32)
        mid_cur[...] = jnp.zeros((tm, D_MODEL), F32)

    s = jnp.minimum(i, n_tiles - 1) % tiles_per_seq
    ext_ref[0:halo, :] = jnp.where(s == 0, 0.0, ext_ref[0:halo, :])

    def normalise():
        ext_ref[halo:halo + tm, :] = _rmsnorm(x_ref[...], gmix_ref[...])

    def pool_block(gi, r0, nr):
        w = POOL_WINDOWS[gi]
        cols = slice(gi * POOL_GC, (gi + 1) * POOL_GC)
        pos = s * tm + r0 + lax.broadcasted_iota(jnp.int32, (nr, 1), 0)
        cur = ext_ref[halo + r0:halo + r0 + nr, cols]
        acc = cur
        for k in range(1, w):
            acc = acc + ext_ref[halo + r0 - k:halo + r0 - k + nr, cols]
        cnt = jnp.minimum(pos + 1, w).astype(F32)
        d = (acc / cnt - cur).astype(BF16)
        y = jnp.dot(d, w_ref[gi], preferred_element_type=F32)
        mid_next[r0:r0 + nr, cols] = x_ref[r0:r0 + nr, cols] + y * sc_ref[:, cols]

    t = jnp.minimum(i, n_tiles - 1)
    sample, half = t // 2, t % 2

    def buffer_group(g):
        kt, vt = c_refs[g][0, 0, 0], c_refs[g][0, 0, 1]
        lane = lax.broadcasted_iota(jnp.int32, kvt_ref.shape[3:], 1)
        new_k = jnp.sum(jnp.where(lane == sample, kvt_ref[half, g, 0], 0.0), axis=1, keepdims=True)
        new_v = jnp.sum(jnp.where(lane == sample, kvt_ref[half, g, 1], 0.0), axis=1, keepdims=True)
        rk, rv, o_row, lse_row = _sample_attend_and_roll(
            kt, vt, qkn_ref[0, 0, g:g + 1, :], qkn_ref[0, 0, ng + g:ng + g + 1, :],
            qkn_ref[0, 0, 2 * ng + g:2 * ng + g + 1, :], new_k, new_v, buf_lens[g], ATTN_PATTERNS[g][1])
        co_refs[g][0, 0, 0] = rk
        co_refs[g][0, 0, 1] = rv
        ol_ref[0, 0, g:g + 1, :] = o_row
        ol_ref[0, 0, ng + g:ng + g + 1, :] = lse_row

    side = [normalise]
    for gi, splits in ((3, 4), (2, 2), (1, 1), (0, 1)):
        nr = tm // splits
        side += [functools.partial(pool_block, gi, p * nr, nr) for p in range(splits)]
    for g in range(ng):
        side.insert(1 + 3 * g, functools.partial(buffer_group, ng - 1 - g))
    _mlp_residual(mid_cur[...], gmlp_ref[...], wu_ref, wd_ref, a_ref, o_ref, side)

    st_ref[0] = ext_ref[tm + 1:tm + halo, :]
    ext_ref[0:halo, :] = ext_ref[tm:tm + halo, :]
    mid_cur[...] = mid_next[...]


def _pool_mlp(x, gmix, w, sc, gmlp, wu, wd, layer, tm, caches, li, prev_rolled, qkn, kvt):
    b, s, _ = x.shape
    tps = s // tm
    n_tiles = b * tps
    nb = qkn.shape[0]
    hw = GROUP_W // 2
    assert n_tiles == 2 * nb, "one half of one sample's heads rides along with every row tile"
    ng = len(caches)
    buf_lens = tuple(c.shape[-1] for c in caches)
    aliased = prev_rolled is not None
    x_spec, o_spec = _skew_specs(n_tiles, tm)
    const = lambda i: (0, 0)

    def half_sample(i):
        t = jnp.minimum(i, n_tiles - 1)
        return t // 2, t % 2

    def cache_map(i):
        smp, half = half_sample(i)
        return (li, smp, 0, half, 0)

    def row_map(i):
        smp, half = half_sample(i)
        return (smp, half, 0, 0)

    cblk = [pl.BlockSpec((1, 1, 2, hw, n), cache_map) for n in buf_lens]
    in_specs = [
        x_spec,
        pl.BlockSpec((1, D_MODEL), const),
        pl.BlockSpec((len(POOL_WINDOWS), POOL_GC, POOL_GC), lambda i: (0, 0, 0)),
        pl.BlockSpec((1, D_MODEL), const),
    ] + _mlp_specs() + cblk + [
        pl.BlockSpec((1, 1, 3 * ng, hw), row_map),
        pl.BlockSpec(kvt.shape, lambda i: (0, 0, 0, 0, 0)),
    ]
    args = [x.reshape(b * s, D_MODEL), gmix, w, sc, gmlp, wu[layer], wd[layer], *caches, qkn, kvt]
    aliases = {}
    if aliased:
        aliases = {len(args) + g: 2 + g for g in range(ng)}
        in_specs += [pl.BlockSpec(memory_space=pl.ANY)] * ng
        args += list(prev_rolled)
    res = pl.pallas_call(
        functools.partial(_pool_mlp_kernel, tm=tm, tiles_per_seq=tps, n_tiles=n_tiles,
                          buf_lens=buf_lens, aliased=aliased),
        grid=(n_tiles + 1,),
        in_specs=in_specs,
        out_specs=[
            o_spec,
            pl.BlockSpec((1, POOL_STATE, D_MODEL),
                         lambda i: (jnp.minimum(i, n_tiles - 1) // tps, 0, 0)),
        ] + cblk + [pl.BlockSpec((1, 1, 2 * ng, hw), row_map)],
        out_shape=[
            jax.ShapeDtypeStruct((b * s, D_MODEL), F32),
            jax.ShapeDtypeStruct((b, POOL_STATE, D_MODEL), F32),
        ] + [jax.ShapeDtypeStruct(c.shape, F32) for c in caches]
        + [jax.ShapeDtypeStruct((nb, 2, 2 * ng, hw), F32)],
        input_output_aliases=aliases,
        scratch_shapes=[pltpu.VMEM((tm + POOL_STATE + 1, D_MODEL), F32)] + _skew_scratch(tm),
        compiler_params=_params("arbitrary"),
        name="pool_mlp",
    )(*args)
    return res[0].reshape(b, s, D_MODEL), res[1], list(res[2:2 + ng]), res[2 + ng]


def _pool_sample_kernel(x_ref, st_ref, g_ref, w_ref, sc_ref, o_ref, nst_ref, *, pos):
    x = x_ref[...]
    xn = _rmsnorm(x, g_ref[...])
    for gi, w in enumerate(POOL_WINDOWS):
        cols = slice(gi * POOL_GC, (gi + 1) * POOL_GC)
        cur = xn[:, cols]
        acc = cur
        for k in range(1, w):
            acc = acc + st_ref[POOL_STATE - k, :, cols]
        d = acc / float(min(pos + 1, w)) - cur
        y = jnp.dot(d.astype(BF16), w_ref[gi], preferred_element_type=F32)
        o_ref[:, cols] = x[:, cols] + y * sc_ref[:, cols]
    for k in range(POOL_STATE - 1):
        nst_ref[k] = st_ref[k + 1]
    nst_ref[POOL_STATE - 1] = xn


def _pool_sample(x, st, g, w, sc, pos):
    n = x.shape[0]
    return pl.pallas_call(
        functools.partial(_pool_sample_kernel, pos=pos),
        out_shape=[
            jax.ShapeDtypeStruct((n, D_MODEL), F32),
            jax.ShapeDtypeStruct((POOL_STATE, n, D_MODEL), F32),
        ],
        compiler_params=pltpu.CompilerParams(vmem_limit_bytes=VMEM_LIMIT),
        name="pool_sample",
    )(x, st, g, w, sc)


def _rope_tables(pos):
    half = HEAD_DIM // 2
    inv = ROPE_THETA ** (-np.arange(half, dtype=np.float64) * 2.0 / HEAD_DIM)
    ang = pos.astype(np.float64)[:, None] * inv[None, :]
    cos, sin = np.cos(ang), np.sin(ang)
    return (jnp.asarray(np.concatenate([cos, cos, cos, cos], axis=1), F32),
            jnp.asarray(np.concatenate([-sin, sin, -sin, sin], axis=1), F32))


def _gain_tables(gain):
    half = HEAD_DIM // 2
    swapped = jnp.concatenate([gain[:, half:], gain[:, :half]], axis=1)
    return jnp.concatenate([gain, gain], axis=1), jnp.concatenate([swapped, swapped], axis=1)


def _qkv_kernel(x_ref, g_ref, w_ref, gq_ref, gqs_ref, gk_ref, gks_ref, cos_ref, sin_ref, ones_ref,
                *rest, tm, dils, out_dtype, tail_lens, n_prev, n_cast):
    cast_in = rest[n_prev:n_prev + n_cast]
    rest = rest[n_prev + n_cast:]
    n_out = 3 * N_GROUPS if dils else 1
    outs = rest[:n_out]
    tails = rest[n_out:n_out + len(tail_lens)]
    cast_out = rest[n_out + len(tail_lens):n_out + len(tail_lens) + n_cast]
    stage_ref = rest[n_out + len(tail_lens) + n_cast]
    kv_stage = rest[-1] if tail_lens else None
    for src, dst in zip(cast_in, cast_out):
        dst[...] = src[...].astype(BF16)
    x = x_ref[0]
    h = _rmsnorm(x, g_ref[...]).astype(BF16)
    cosf = cos_ref[...]
    sinf = sin_ref[...]
    lane = lax.broadcasted_iota(jnp.int32, (tm, LANES), 1)
    first_half = (lane % HEAD_DIM) < (HEAD_DIM // 2)
    units = [(part, gi, hh) for part in range(3) for gi in range(N_GROUPS)
             for hh in range(GROUP_W // MXU_DIM)]

    def project(part, gi, hh):
        c0 = (part * N_GROUPS + gi) * GROUP_W + hh * MXU_DIM
        return jnp.dot(h, w_ref[:, c0:c0 + MXU_DIM], preferred_element_type=F32)

    def rope_tables(part, gi):
        ta = (gq_ref, gk_ref)[part][gi:gi + 1, :] * cosf
        tb = (gqs_ref, gks_ref)[part][gi:gi + 1, :] * sinf
        return (ta * Q_SCALE, tb * Q_SCALE) if part == 0 else (ta, tb)

    ahead = [project(*u) for u in units[:QKV_LOOKAHEAD]]
    for idx, (part, gi, hh) in enumerate(units):
        res = ahead.pop(0)
        if idx + QKV_LOOKAHEAD < len(units):
            ahead.append(project(*units[idx + QKV_LOOKAHEAD]))
        slabs = [res[:, sl * LANES:(sl + 1) * LANES] for sl in range(MXU_DIM // LANES)]
        if part < 2:
            if hh == 0:
                ta, tb = rope_tables(part, gi)
            ss = jnp.dot((res * res).astype(BF16), ones_ref[...], preferred_element_type=F32)
            r = lax.rsqrt(ss * (1.0 / HEAD_DIM) + EPS)
            for sl, xs in enumerate(slabs):
                swapped = jnp.where(first_half, pltpu.roll(xs, LANES - HEAD_DIM // 2, 1),
                                    pltpu.roll(xs, HEAD_DIM // 2, 1))
                slabs[sl] = r[:, sl * LANES:(sl + 1) * LANES] * (xs * ta + swapped * tb)
        out_ref = outs[part * N_GROUPS + gi] if dils else None
        d = dils[gi] if dils else 0
        for sl, val in enumerate(slabs):
            c = hh * (MXU_DIM // LANES) + sl
            lanes = slice(c * LANES, (c + 1) * LANES)
            if tail_lens and part > 0:
                t0 = (gi * 2 + part - 1) * GROUP_W
                kv_stage[:, t0 + c * LANES:t0 + (c + 1) * LANES] = val
            if not dils:
                outs[0][part * N_GROUPS + gi, :, lanes] = val.astype(out_dtype)
            elif d == 1:
                out_ref[0, 0, :, lanes] = val.astype(out_dtype)
            else:
                stage_ref[c] = val
                for r_ in range(d):
                    out_ref[0, r_, :, lanes] = stage_ref[c, pl.ds(r_, tm // d, stride=d), :].astype(out_dtype)

    j = pl.program_id(1)
    for gi, n in enumerate(tail_lens):
        rows = min(n, tm)

        @pl.when(j >= pl.num_programs(1) - max(n // tm, 1))
        def _(gi=gi, rows=rows):
            for part in range(2):
                t0 = (gi * 2 + part) * GROUP_W
                tails[gi][0, 0, part * GROUP_W:(part + 1) * GROUP_W, :] = (
                    kv_stage[tm - rows:tm, t0:t0 + GROUP_W].T)


def _qkv(x, g, w, li, gq, gk, cosf, sinf, ones, *, tm, dils, out_dtype, tail_lens=(), prev_tails=None,
         cast=()):
    b, s, _ = x.shape
    nt = s // tm
    gq_t, gq_s = gq
    gk_t, gk_s = gk
    const2 = lambda i, j: (0, 0)
    in_specs = [
        pl.BlockSpec((1, tm, D_MODEL), lambda i, j: (i, j, 0)),
        pl.BlockSpec((1, D_MODEL), const2),
        pl.BlockSpec((D_MODEL, QKV_W), const2),
        pl.BlockSpec((N_GROUPS, LANES), const2),
        pl.BlockSpec((N_GROUPS, LANES), const2),
        pl.BlockSpec((N_GROUPS, LANES), const2),
        pl.BlockSpec((N_GROUPS, LANES), const2),
        pl.BlockSpec((tm, LANES), lambda i, j: (j, 0)),
        pl.BlockSpec((tm, LANES), lambda i, j: (j, 0)),
        pl.BlockSpec((MXU_DIM, MXU_DIM), const2),
    ]
    out_specs, out_shape = [], []
    for _ in range(3):
        for d in dils:
            out_specs.append(pl.BlockSpec((1, d, tm // d, GROUP_W), lambda i, j: (i, 0, j, 0)))
            out_shape.append(jax.ShapeDtypeStruct((b, d, s // d, GROUP_W), out_dtype))
    if not dils:
        assert b == 1 and not tail_lens
        out_specs.append(pl.BlockSpec((3 * N_GROUPS, tm, GROUP_W), lambda i, j: (0, j, 0)))
        out_shape.append(jax.ShapeDtypeStruct((3 * N_GROUPS, s, GROUP_W), out_dtype))
    args = [x, g, w, gq_t, gq_s, gk_t, gk_s, cosf, sinf, ones]
    aliases = {}
    if prev_tails is not None:
        aliases = {len(args) + gi: 9 + gi for gi in range(len(tail_lens))}
        in_specs += [pl.BlockSpec(memory_space=pl.ANY)] * len(tail_lens)
        args += list(prev_tails)
    for n in tail_lens:
        first = nt - max(n // tm, 1)
        out_specs.append(pl.BlockSpec((1, 1, 2 * GROUP_W, min(n, tm)),
                                      lambda i, j, first=first: (li, i, 0, jnp.maximum(j - first, 0))))
        out_shape.append(jax.ShapeDtypeStruct((DEPTH // 2, b, 2 * GROUP_W, n), F32))
    steps = b * nt
    for arr, layer in cast:
        rows = arr.shape[1] // steps
        in_specs.append(pl.BlockSpec((None, rows, arr.shape[2]),
                                     lambda i, j, layer=layer: (layer, i * nt + j, 0)))
        out_specs.append(pl.BlockSpec((rows, arr.shape[2]), lambda i, j: (i * nt + j, 0)))
        out_shape.append(jax.ShapeDtypeStruct(arr.shape[1:], BF16))
        args.append(arr)
    scratch = [pltpu.VMEM((GROUP_W // LANES, tm, LANES), F32)]
    if tail_lens:
        scratch.append(pltpu.VMEM((tm, 2 * N_GROUPS * GROUP_W), F32))
    return pl.pallas_call(
        functools.partial(_qkv_kernel, tm=tm, dils=dils, out_dtype=out_dtype, tail_lens=tuple(tail_lens),
                          n_prev=len(args) - 10 - len(cast), n_cast=len(cast)),
        grid=(b, nt),
        in_specs=in_specs,
        out_specs=out_specs,
        out_shape=out_shape,
        input_output_aliases=aliases,
        scratch_shapes=scratch,
        compiler_params=_params("arbitrary", "arbitrary"),
        name="qkv",
    )(*args)


def _stat_is_max(shape):
    lane = lax.broadcasted_iota(jnp.int32, shape, len(shape) - 1)
    return (lane % HEAD_DIM) < (HEAD_DIM // 2)


def _unpack_stats(packed):
    is_max = _stat_is_max(packed.shape)
    half = HEAD_DIM // 2
    m = jnp.where(is_max, packed, pltpu.roll(packed, half, 1))
    l = jnp.where(is_max, pltpu.roll(packed, LANES - half, 1), packed)
    return m, l


def _attn_kernel(q_ref, kp_ref, kc_ref, vp_ref, vc_ref, o_ref, ml_ref, kcat_ref, vcat_ref, p_scr, m_scr,
                 *, tq):
    j = pl.program_id(1)
    kcat_ref[0:Q_SUB, :] = kp_ref[0]
    kcat_ref[Q_SUB:Q_SUB + tq, :] = kc_ref[0]
    vcat_ref[0:Q_SUB, :] = vp_ref[0]
    vcat_ref[Q_SUB:Q_SUB + tq, :] = vc_ref[0]

    nk = 2 * Q_SUB
    nq = 2 * Q_SUB
    row = lax.broadcasted_iota(jnp.int32, (nq, nk), 0) & (Q_SUB - 1)
    col = lax.broadcasted_iota(jnp.int32, (nq, nk), 1)
    dist = row + Q_SUB - col
    band = (dist >= 0) & (dist <= N_KEYS_M1)
    lane = lax.broadcasted_iota(jnp.int32, (Q_SUB, LANES), 1)
    low_head = lane < HEAD_DIM
    stat_is_max = _stat_is_max((nq, LANES))
    zero_q = jnp.zeros((Q_SUB, LANES), BF16)
    ones_v = jnp.ones((nk, LANES), BF16)

    first_valid = band & ((col >= Q_SUB) | (j > 0))
    jobs = [(sb * Q_SUB, slice(p * LANES, (p + 1) * LANES))
            for sb in range(tq // Q_SUB) for p in range(GROUP_W // LANES)]

    def probabilities(idx):
        r0, lanes = jobs[idx]
        qp = q_ref[0, r0:r0 + Q_SUB, lanes]
        q2 = jnp.concatenate([jnp.where(low_head, qp, zero_q), jnp.where(low_head, zero_q, qp)], axis=0)
        s = lax.dot_general(q2, kcat_ref[r0:r0 + nk, lanes], (((1,), (1,)), ((), ())),
                            preferred_element_type=F32)
        s = jnp.where(first_valid if r0 == 0 else band, s, NEG)
        m = jnp.max(s, axis=1, keepdims=True)
        p_scr[idx % ATTN_STAGES] = jnp.exp2(s - m).astype(BF16)
        m_scr[idx % ATTN_STAGES] = jnp.broadcast_to(m, (nq, LANES))

    def outputs(idx):
        r0, lanes = jobs[idx]
        vaug = jnp.concatenate([vcat_ref[r0:r0 + nk, lanes], ones_v], axis=1)
        ov = jnp.dot(p_scr[idx % ATTN_STAGES], vaug, preferred_element_type=F32)
        num = ov[:, :LANES]
        ml = jnp.where(stat_is_max, m_scr[idx % ATTN_STAGES], ov[:, LANES:])
        o_ref[0, r0:r0 + Q_SUB, lanes] = jnp.where(low_head, num[:Q_SUB], num[Q_SUB:]).astype(BF16)
        ml_ref[0, r0:r0 + Q_SUB, lanes] = jnp.where(low_head, ml[:Q_SUB], ml[Q_SUB:])

    lag = ATTN_STAGES - 1
    for idx in range(len(jobs) + lag):
        if idx < len(jobs):
            probabilities(idx)
        if idx >= lag:
            outputs(idx - lag)


def _attn(q, k, v, tq):
    nseq, length, _ = q.shape
    ratio = tq // Q_SUB
    cur = lambda i, j: (i, j, 0)
    prev = lambda i, j: (i, jnp.maximum(j * ratio - 1, 0), 0)
    return pl.pallas_call(
        functools.partial(_attn_kernel, tq=tq),
        grid=(nseq, length // tq),
        in_specs=[
            pl.BlockSpec((1, tq, GROUP_W), cur),
            pl.BlockSpec((1, Q_SUB, GROUP_W), prev),
            pl.BlockSpec((1, tq, GROUP_W), cur),
            pl.BlockSpec((1, Q_SUB, GROUP_W), prev),
            pl.BlockSpec((1, tq, GROUP_W), cur),
        ],
        out_specs=[pl.BlockSpec((1, tq, GROUP_W), cur), pl.BlockSpec((1, tq, GROUP_W), cur)],
        out_shape=[jax.ShapeDtypeStruct((nseq, length, GROUP_W), BF16),
                   jax.ShapeDtypeStruct((nseq, length, GROUP_W), F32)],
        scratch_shapes=[pltpu.VMEM((tq + Q_SUB, GROUP_W), BF16)] * 2
        + [pltpu.VMEM((ATTN_STAGES, 2 * Q_SUB, 2 * Q_SUB), BF16),
           pltpu.VMEM((ATTN_STAGES, 2 * Q_SUB, LANES), F32)],
        compiler_params=_params("arbitrary", "arbitrary"),
        name="attn_prompt",
    )(q, k, k, v, v)


def _merge_pieces(load_x, store_y, o_refs, l_refs, wo_ref, und, tm, dils):
    nc = GROUP_W // LANES

    def slabs(ref, d, scr, r0, nr):
        if d == 1:
            return [ref[0, 0, r0:r0 + nr, c * LANES:(c + 1) * LANES] for c in range(nc)]
        for r_ in range(d):
            for c in range(nc):
                scr[c, pl.ds(r0 + r_, nr // d, stride=d), :] = (
                    ref[0, r_, r0 // d:(r0 + nr) // d, c * LANES:(c + 1) * LANES].astype(F32))
        return [scr[c, r0:r0 + nr, :] for c in range(nc)]

    def merge_rows(r0, nr):
        os_ = [slabs(o_refs[g], dils[g], und[2 * g], r0, nr) for g in range(N_GROUPS)]
        ss_ = [slabs(l_refs[g], dils[g], und[2 * g + 1], r0, nr) for g in range(N_GROUPS)]
        merged = []
        for c in range(nc):
            ms, ls_ = zip(*[_unpack_stats(ss_[g][c]) for g in range(N_GROUPS)])
            mx = jnp.maximum(jnp.maximum(ms[0], ms[1]), ms[2])
            ws = [jnp.exp2(m - mx) for m in ms]
            num = ws[0] * os_[0][c] + ws[1] * os_[1][c] + ws[2] * os_[2][c]
            den = ws[0] * ls_[0] + ws[1] * ls_[1] + ws[2] * ls_[2]
            merged.append((num / den).astype(BF16))
        merged = jnp.concatenate(merged, axis=1)
        store_y(r0, nr, load_x(r0, nr) + jnp.dot(merged, wo_ref[...], preferred_element_type=F32))

    nr = min(tm, Q_SUB)
    return [functools.partial(merge_rows, r0, nr) for r0 in range(0, tm, nr)]


def _merge_scratch(tm):
    return [pltpu.VMEM((GROUP_W // LANES, tm, LANES), F32)] * (2 * N_GROUPS)


def _merge_kernel(x_ref, o0, o1, o2, l0, l1, l2, wo_ref, y_ref, *scratch, tm, dils):
    def store_y(r0, nr, v):
        y_ref[0, r0:r0 + nr, :] = v

    for piece in _merge_pieces(lambda r0, nr: x_ref[0, r0:r0 + nr, :], store_y, (o0, o1, o2),
                               (l0, l1, l2), wo_ref, scratch, tm, dils):
        piece()


def _merge_mlp_kernel(x_ref, o0, o1, o2, l0, l1, l2, wo_ref, gmlp_ref, wu_ref, wd_ref, y_ref,
                      *scratch, tm, dils):
    und = scratch[:2 * N_GROUPS]
    mid_next, mid_cur, a_ref = scratch[2 * N_GROUPS:]

    @pl.when(pl.program_id(0) == 0)
    def _():
        mid_cur[...] = jnp.zeros((tm, D_MODEL), F32)

    def store_mid(r0, nr, v):
        mid_next[r0:r0 + nr, :] = v

    side = _merge_pieces(lambda r0, nr: x_ref[r0:r0 + nr, :], store_mid, (o0, o1, o2), (l0, l1, l2),
                         wo_ref, und, tm, dils)
    _mlp_residual(mid_cur[...], gmlp_ref[...], wu_ref, wd_ref, a_ref, y_ref, side, side_offset=1)
    mid_cur[...] = mid_next[...]


def _merge_mlp(x, os_, ls_, wo, li, gmlp, wu, wd, layer, *, tm, dils):
    b, s, _ = x.shape
    tps = s // tm
    n_tiles = b * tps
    x_spec, o_spec = _skew_specs(n_tiles, tm)

    def grp_map(i):
        t = jnp.minimum(i, n_tiles - 1)
        return (t // tps, 0, t % tps, 0)

    grp = [pl.BlockSpec((1, d, tm // d, GROUP_W), grp_map) for d in dils]
    y = pl.pallas_call(
        functools.partial(_merge_mlp_kernel, tm=tm, dils=dils),
        grid=(n_tiles + 1,),
        in_specs=[x_spec] + grp + grp
        + [pl.BlockSpec((None, GROUP_W, D_MODEL), lambda i: (li, 0, 0))] + _mlp_specs(),
        out_specs=o_spec,
        out_shape=jax.ShapeDtypeStruct((b * s, D_MODEL), F32),
        scratch_shapes=_merge_scratch(tm) + _skew_scratch(tm),
        compiler_params=_params("arbitrary"),
        name="merge_mlp",
    )(x.reshape(b * s, D_MODEL), *os_, *ls_, wo, gmlp, wu[layer], wd[layer])
    return y.reshape(b, s, D_MODEL)


def _merge(x, attn, wo, li, *, tm):
    b, s, _ = x.shape
    dils = (1,) * N_GROUPS
    grp = [pl.BlockSpec((None, 1, 1, tm, GROUP_W), lambda i, j, k=k: (k, i, 0, j, 0))
           for k in range(2 * N_GROUPS)]
    return pl.pallas_call(
        functools.partial(_merge_kernel, tm=tm, dils=dils),
        grid=(b, s // tm),
        in_specs=[pl.BlockSpec((1, tm, D_MODEL), lambda i, j: (i, j, 0))] + grp
        + [pl.BlockSpec((None, GROUP_W, D_MODEL), lambda i, j: (li, 0, 0))],
        out_specs=pl.BlockSpec((1, tm, D_MODEL), lambda i, j: (i, j, 0)),
        out_shape=jax.ShapeDtypeStruct((b, s, D_MODEL), F32),
        scratch_shapes=_merge_scratch(tm),
        compiler_params=_params("arbitrary", "arbitrary"),
        name="merge_out_proj",
    )(x, *([attn] * (2 * N_GROUPS)), wo)


def _sample_attend_and_roll(kt, vt, q_row, kn_row, vn_row, new_k, new_v, n, dil):
    hw = kt.shape[0]
    nh = hw // HEAD_DIM
    sub = lax.broadcasted_iota(jnp.int32, (nh, hw), 0)
    ln = lax.broadcasted_iota(jnp.int32, (nh, hw), 1)
    own = (ln // HEAD_DIM) == sub
    e = lax.broadcasted_iota(jnp.int32, (nh, n), 1)
    qb = jnp.where(own, jnp.broadcast_to(q_row, (nh, hw)), 0.0)
    s = jnp.dot(qb.astype(BF16), kt.astype(BF16), preferred_element_type=F32)
    s = jnp.where((n - e) % dil == 0, s, NEG)
    s_new = jnp.sum(qb * kn_row, axis=1, keepdims=True)
    m = jnp.maximum(jnp.max(s, axis=1, keepdims=True), s_new)
    pr = jnp.exp2(s - m)
    p_new = jnp.exp2(s_new - m)
    l = jnp.sum(pr, axis=1, keepdims=True) + p_new
    o = lax.dot_general(pr.astype(BF16), vt.astype(BF16), (((1,), (1,)), ((), ())),
                        preferred_element_type=F32)
    o = o + p_new * vn_row
    o_row = jnp.sum(jnp.where(own, o, 0.0), axis=0, keepdims=True)
    stats = jnp.where(_stat_is_max((nh, hw)), m, l)
    stats_row = jnp.sum(jnp.where(own, stats, 0.0), axis=0, keepdims=True)
    return _shift_in(kt, new_k), _shift_in(vt, new_v), o_row, stats_row


def _shift_in(x, new_col):
    rows, n = x.shape
    lane0 = lax.broadcasted_iota(jnp.int32, (rows, LANES), 1) == 0
    out = []
    for j in range(n // LANES):
        cur = x[:, j * LANES:(j + 1) * LANES]
        nxt = x[:, (j + 1) * LANES:(j + 2) * LANES] if (j + 1) * LANES < n else new_col
        out.append(pltpu.roll(jnp.where(lane0, nxt, cur), LANES - 1, 1))
    return jnp.concatenate(out, axis=1)


def kernel(x_prompt, x_sample, state_pool, cache_kv_w128, cache_kv_w512, cache_kv_w2048, norm_mix,
           norm_mlp, pool_w, pool_scale, attn_w_qkv, attn_q_norm, attn_k_norm, attn_w_o, mlp_w_up,
           mlp_w_down):
    bsz, seq, _ = x_prompt.shape
    nb = x_sample.shape[0]
    past = PAST_LEN
    dils = tuple(d for _, d in ATTN_PATTERNS)

    wu = [mlp_w_up[0].astype(BF16)]
    wd = [mlp_w_down[0].astype(BF16)]
    wqkv = [attn_w_qkv[0].astype(BF16)]
    later = ([(mlp_w_up, l) for l in range(1, DEPTH)] + [(mlp_w_down, l) for l in range(1, DEPTH)]
             + [(attn_w_qkv, l) for l in range(1, DEPTH // 2)])
    wo = attn_w_o.astype(BF16)
    wp = pool_w.astype(BF16)
    head = lax.broadcasted_iota(jnp.int32, (MXU_DIM, MXU_DIM), 0) // HEAD_DIM
    ones = (head == head.T).astype(BF16)
    cos_p, sin_p = _rope_tables(np.arange(seq))
    cos_s, sin_s = _rope_tables(np.full((nb,), past))

    caches = [jnp.transpose(c, (0, 1, 3, 4, 5, 2)).reshape(c.shape[0], nb, 2, GROUP_W, c.shape[2])
              for c in (cache_kv_w128, cache_kv_w512, cache_kv_w2048)]

    xp = x_prompt
    xs = x_sample.reshape(nb, D_MODEL)
    pool_p, pool_s = [], []
    new_caches = kv_tails = None
    hw = GROUP_W // 2
    for layer in range(DEPTH):
        li = layer // 2
        g_mix = norm_mix[layer][None, :]
        g_mlp = norm_mlp[layer][None, :]
        gq = _gain_tables(attn_q_norm[li])
        gk = _gain_tables(attn_k_norm[li])
        if layer % 2 == 0:
            sc = pool_scale[li][None, :]
            st_in = jnp.transpose(state_pool[li], (1, 0, 2))
            xs, st = _pool_sample(xs, st_in, g_mix, wp[li], sc, past)
            pool_s.append(jnp.transpose(st, (1, 0, 2)))
            xs = _mlp(xs, g_mlp, wu, wd, layer, nb)
            packed = _qkv(xs[None], norm_mix[layer + 1][None, :], wqkv[li], li, gq, gk, cos_s, sin_s, ones,
                          tm=nb, dils=(), out_dtype=F32)[0]
            qkn = jnp.transpose(packed.reshape(9, nb, 2, hw), (1, 2, 0, 3))
            kvt = jnp.transpose(packed[N_GROUPS:].reshape(2, N_GROUPS, nb, 2, hw), (3, 1, 0, 4, 2))
            xp, st, new_caches, sample_attn = _pool_mlp(xp, g_mix, wp[li], sc, g_mlp, wu, wd, layer,
                                                        POOL_TILE, caches, li, new_caches, qkn, kvt)
            pool_p.append(st)
        else:
            outs = _qkv(xp, g_mix, wqkv[li], li, gq, gk, cos_p, sin_p, ones, tm=ROW_TILE, dils=dils,
                        out_dtype=BF16, tail_lens=tuple(min(w, seq) for w, _ in ATTN_PATTERNS),
                        prev_tails=kv_tails, cast=later if li == 0 else ())
            kv_tails = outs[9:9 + N_GROUPS]
            if li == 0:
                cast_w = list(outs[9 + N_GROUPS:])
                wu += cast_w[:DEPTH - 1]
                wd += cast_w[DEPTH - 1:2 * (DEPTH - 1)]
                wqkv += cast_w[2 * (DEPTH - 1):]
            os_, ls_ = [], []
            for gi, (w, d) in enumerate(ATTN_PATTERNS):
                q, k, v = (outs[part * N_GROUPS + gi].reshape(bsz * d, seq // d, GROUP_W)
                           for part in range(3))
                o, stats = _attn(q, k, v, min(ATTN_TQ, seq // d))
                os_.append(o.reshape(bsz, d, seq // d, GROUP_W))
                ls_.append(stats.reshape(bsz, d, seq // d, GROUP_W))
            xp = _merge_mlp(xp, os_, ls_, wo, li, g_mlp, wu, wd, layer, tm=ROW_TILE, dils=dils)
            attn = jnp.transpose(sample_attn, (2, 0, 1, 3)).reshape(2 * N_GROUPS, 1, 1, nb, GROUP_W)
            xs = _merge(xs[None], attn, wo, li, tm=nb)[0]
            xs = _mlp(xs, g_mlp, wu, wd, layer, nb)

    kv_s, kv_p = ([jnp.transpose(c.reshape(c.shape[0], c.shape[1], 2, HEADS, HEAD_DIM, c.shape[-1]),
                                 (0, 1, 5, 2, 3, 4)) for c in group] for group in (new_caches, kv_tails))
    return (xp, xs.reshape(nb, 1, D_MODEL), jnp.stack(pool_p), jnp.stack(pool_s),
            kv_p[0], kv_s[0], kv_p[1], kv_s[1], kv_p[2], kv_s[2])
```

```python
import functools

import jax
import jax.numpy as jnp
import numpy as np
from jax import lax
from jax.experimental import pallas as pl
from jax.experimental.pallas import tpu as pltpu

F32 = jnp.float32
BF16 = jnp.bfloat16

D_MODEL = 1024
D_FF = 4 * D_MODEL
DEPTH = 4
POOL_WINDOWS = (2, 4, 8, 16)
POOL_GC = D_MODEL // len(POOL_WINDOWS)
POOL_STATE = max(POOL_WINDOWS) - 1
ATTN_PATTERNS = ((128, 1), (512, 4), (2048, 16))
N_GROUPS = len(ATTN_PATTERNS)
HEAD_DIM = 64
HEADS = 8
GROUP_W = HEADS * HEAD_DIM
QKV_W = 3 * N_GROUPS * GROUP_W
N_KEYS_M1 = 128
ROPE_THETA = 10000.0
PAST_LEN = 8192
EPS = 1e-6
NEG = -1e30
LOG2E = 1.4426950408889634
Q_SCALE = HEAD_DIM ** -0.5 * LOG2E

LANES = 128
MXU_DIM = 256
VMEM_LIMIT = 56 * 1024 * 1024
ROW_TILE = 512
POOL_TILE = 256
FF_CHUNK = 512
DOWN_CHUNK = 256
ATTN_TQ = 1024
Q_SUB = 128
QKV_LOOKAHEAD = 1
ATTN_STAGES = 3


def _params(*sem):
    return pltpu.CompilerParams(dimension_semantics=sem, vmem_limit_bytes=VMEM_LIMIT)


def _rmsnorm(x, g):
    return x * lax.rsqrt(jnp.mean(x * x, axis=-1, keepdims=True) + EPS) * g


def _mlp_residual(x, g, wu_ref, wd_ref, a_ref, o_ref, side=(), side_offset=0):
    h = _rmsnorm(x, g).astype(BF16)
    n_up = D_FF // FF_CHUNK
    n_down = D_MODEL // DOWN_CHUNK
    slots = n_up + n_down - 1

    def run_side(slot):
        for j, piece in enumerate(side):
            if min((2 * j + side_offset) * slots // (2 * len(side)), slots - 1) == slot:
                piece()

    for c in range(n_up):
        cols = slice(c * FF_CHUNK, (c + 1) * FF_CHUNK)
        u = jnp.dot(h, wu_ref[:, cols], preferred_element_type=F32)
        a_ref[:, cols] = jnp.square(jnp.maximum(u, 0.0)).astype(BF16)
        run_side(c)
    for c in range(n_down):
        cols = slice(c * DOWN_CHUNK, (c + 1) * DOWN_CHUNK)
        o_ref[:, cols] = x[:, cols] + jnp.dot(a_ref[...], wd_ref[:, cols], preferred_element_type=F32)
        if c < n_down - 1:
            run_side(n_up + c)


def _mlp_kernel(x_ref, g_ref, wu_ref, wd_ref, o_ref, a_ref):
    _mlp_residual(x_ref[...], g_ref[...], wu_ref, wd_ref, a_ref, o_ref)


def _mlp(x, g, wu, wd, layer, tm):
    n = x.shape[0]
    return pl.pallas_call(
        _mlp_kernel,
        grid=(n // tm,),
        in_specs=[
            pl.BlockSpec((tm, D_MODEL), lambda i: (i, 0)),
            pl.BlockSpec((1, D_MODEL), lambda i: (0, 0)),
            pl.BlockSpec((D_MODEL, D_FF), lambda i: (0, 0)),
            pl.BlockSpec((D_FF, D_MODEL), lambda i: (0, 0)),
        ],
        out_specs=pl.BlockSpec((tm, D_MODEL), lambda i: (i, 0)),
        out_shape=jax.ShapeDtypeStruct((n, D_MODEL), F32),
        scratch_shapes=[pltpu.VMEM((tm, D_FF), BF16)],
        compiler_params=_params("arbitrary"),
        name="mlp",
    )(x, g, wu[layer], wd[layer])


def _skew_specs(n_tiles, tm):
    x_spec = pl.BlockSpec((tm, D_MODEL), lambda i: (jnp.minimum(i, n_tiles - 1), 0))
    o_spec = pl.BlockSpec((tm, D_MODEL), lambda i: (jnp.maximum(i - 1, 0), 0))
    return x_spec, o_spec


def _skew_scratch(tm):
    return [pltpu.VMEM((tm, D_MODEL), F32), pltpu.VMEM((tm, D_MODEL), F32), pltpu.VMEM((tm, D_FF), BF16)]


def _mlp_specs():
    return [pl.BlockSpec((1, D_MODEL), lambda i: (0, 0)),
            pl.BlockSpec((D_MODEL, D_FF), lambda i: (0, 0)),
            pl.BlockSpec((D_FF, D_MODEL), lambda i: (0, 0))]


def _pool_mlp_kernel(x_ref, gmix_ref, w_ref, sc_ref, gmlp_ref, wu_ref, wd_ref, *rest,
                     tm, tiles_per_seq, n_tiles, buf_lens, aliased):
    ng = len(buf_lens)
    c_refs, (qkn_ref, kvt_ref) = rest[:ng], rest[ng:ng + 2]
    outs = rest[ng + 2 + (ng if aliased else 0):]
    o_ref, st_ref = outs[:2]
    co_refs, ol_ref = outs[2:2 + ng], outs[2 + ng]
    ext_ref, mid_next, mid_cur, a_ref = outs[3 + ng:]
    i = pl.program_id(0)
    halo = POOL_STATE + 1

    @pl.when(i == 0)
    def _():
        ext_ref[0:halo, :] = jnp.zeros((halo, D_MODEL), F32)
        mid_cur[...] = jnp.zeros((tm, D_MODEL), F32)

    s = jnp.minimum(i, n_tiles - 1) % tiles_per_seq
    ext_ref[0:halo, :] = jnp.where(s == 0, 0.0, ext_ref[0:halo, :])

    def normalise():
        ext_ref[halo:halo + tm, :] = _rmsnorm(x_ref[...], gmix_ref[...])

    def pool_block(gi, r0, nr):
        w = POOL_WINDOWS[gi]
        cols = slice(gi * POOL_GC, (gi + 1) * POOL_GC)
        pos = s * tm + r0 + lax.broadcasted_iota(jnp.int32, (nr, 1), 0)
        cur = ext_ref[halo + r0:halo + r0 + nr, cols]
        acc = cur
        for k in range(1, w):
            acc = acc + ext_ref[halo + r0 - k:halo + r0 - k + nr, cols]
        cnt = jnp.minimum(pos + 1, w).astype(F32)
        d = (acc / cnt - cur).astype(BF16)
        y = jnp.dot(d, w_ref[gi], preferred_element_type=F32)
        mid_next[r0:r0 + nr, cols] = x_ref[r0:r0 + nr, cols] + y * sc_ref[:, cols]

    t = jnp.minimum(i, n_tiles - 1)
    sample, half = t // 2, t % 2

    def buffer_group(g):
        kt, vt = c_refs[g][0, 0, 0], c_refs[g][0, 0, 1]
        lane = lax.broadcasted_iota(jnp.int32, kvt_ref.shape[3:], 1)
        new_k = jnp.sum(jnp.where(lane == sample, kvt_ref[half, g, 0], 0.0), axis=1, keepdims=True)
        new_v = jnp.sum(jnp.where(lane == sample, kvt_ref[half, g, 1], 0.0), axis=1, keepdims=True)
        rk, rv, o_row, lse_row = _sample_attend_and_roll(
            kt, vt, qkn_ref[0, 0, g:g + 1, :], qkn_ref[0, 0, ng + g:ng + g + 1, :],
            qkn_ref[0, 0, 2 * ng + g:2 * ng + g + 1, :], new_k, new_v, buf_lens[g], ATTN_PATTERNS[g][1])
        co_refs[g][0, 0, 0] = rk
        co_refs[g][0, 0, 1] = rv
        ol_ref[0, 0, g:g + 1, :] = o_row
        ol_ref[0, 0, ng + g:ng + g + 1, :] = lse_row

    side = [normalise]
    for gi, splits in ((3, 4), (2, 2), (1, 1), (0, 1)):
        nr = tm // splits
        side += [functools.partial(pool_block, gi, p * nr, nr) for p in range(splits)]
    for g in range(ng):
        side.insert(1 + 3 * g, functools.partial(buffer_group, ng - 1 - g))
    _mlp_residual(mid_cur[...], gmlp_ref[...], wu_ref, wd_ref, a_ref, o_ref, side)

    st_ref[0] = ext_ref[tm + 1:tm + halo, :]
    ext_ref[0:halo, :] = ext_ref[tm:tm + halo, :]
    mid_cur[...] = mid_next[...]


def _pool_mlp(x, gmix, w, sc, gmlp, wu, wd, layer, tm, caches, li, prev_rolled, qkn, kvt):
    b, s, _ = x.shape
    tps = s // tm
    n_tiles = b * tps
    nb = qkn.shape[0]
    hw = GROUP_W // 2
    assert n_tiles == 2 * nb, "one half of one sample's heads rides along with every row tile"
    ng = len(caches)
    buf_lens = tuple(c.shape[-1] for c in caches)
    aliased = prev_rolled is not None
    x_spec, o_spec = _skew_specs(n_tiles, tm)
    const = lambda i: (0, 0)

    def half_sample(i):
        t = jnp.minimum(i, n_tiles - 1)
        return t // 2, t % 2

    def cache_map(i):
        smp, half = half_sample(i)
        return (li, smp, 0, half, 0)

    def row_map(i):
        smp, half = half_sample(i)
        return (smp, half, 0, 0)

    cblk = [pl.BlockSpec((1, 1, 2, hw, n), cache_map) for n in buf_lens]
    in_specs = [
        x_spec,
        pl.BlockSpec((1, D_MODEL), const),
        pl.BlockSpec((len(POOL_WINDOWS), POOL_GC, POOL_GC), lambda i: (0, 0, 0)),
        pl.BlockSpec((1, D_MODEL), const),
    ] + _mlp_specs() + cblk + [
        pl.BlockSpec((1, 1, 3 * ng, hw), row_map),
        pl.BlockSpec(kvt.shape, lambda i: (0, 0, 0, 0, 0)),
    ]
    args = [x.reshape(b * s, D_MODEL), gmix, w, sc, gmlp, wu[layer], wd[layer], *caches, qkn, kvt]
    aliases = {}
    if aliased:
        aliases = {len(args) + g: 2 + g for g in range(ng)}
        in_specs += [pl.BlockSpec(memory_space=pl.ANY)] * ng
        args += list(prev_rolled)
    res = pl.pallas_call(
        functools.partial(_pool_mlp_kernel, tm=tm, tiles_per_seq=tps, n_tiles=n_tiles,
                          buf_lens=buf_lens, aliased=aliased),
        grid=(n_tiles + 1,),
        in_specs=in_specs,
        out_specs=[
            o_spec,
            pl.BlockSpec((1, POOL_STATE, D_MODEL),
                         lambda i: (jnp.minimum(i, n_tiles - 1) // tps, 0, 0)),
        ] + cblk + [pl.BlockSpec((1, 1, 2 * ng, hw), row_map)],
        out_shape=[
            jax.ShapeDtypeStruct((b * s, D_MODEL), F32),
            jax.ShapeDtypeStruct((b, POOL_STATE, D_MODEL), F32),
        ] + [jax.ShapeDtypeStruct(c.shape, F32) for c in caches]
        + [jax.ShapeDtypeStruct((nb, 2, 2 * ng, hw), F32)],
        input_output_aliases=aliases,
        scratch_shapes=[pltpu.VMEM((tm + POOL_STATE + 1, D_MODEL), F32)] + _skew_scratch(tm),
        compiler_params=_params("arbitrary"),
        name="pool_mlp",
    )(*args)
    return res[0].reshape(b, s, D_MODEL), res[1], list(res[2:2 + ng]), res[2 + ng]


def _pool_sample_kernel(x_ref, st_ref, g_ref, w_ref, sc_ref, o_ref, nst_ref, *, pos):
    x = x_ref[...]
    xn = _rmsnorm(x, g_ref[...])
    for gi, w in enumerate(POOL_WINDOWS):
        cols = slice(gi * POOL_GC, (gi + 1) * POOL_GC)
        cur = xn[:, cols]
        acc = cur
        for k in range(1, w):
            acc = acc + st_ref[POOL_STATE - k, :, cols]
        d = acc / float(min(pos + 1, w)) - cur
        y = jnp.dot(d.astype(BF16), w_ref[gi], preferred_element_type=F32)
        o_ref[:, cols] = x[:, cols] + y * sc_ref[:, cols]
    for k in range(POOL_STATE - 1):
        nst_ref[k] = st_ref[k + 1]
    nst_ref[POOL_STATE - 1] = xn


def _pool_sample(x, st, g, w, sc, pos):
    n = x.shape[0]
    return pl.pallas_call(
        functools.partial(_pool_sample_kernel, pos=pos),
        out_shape=[
            jax.ShapeDtypeStruct((n, D_MODEL), F32),
            jax.ShapeDtypeStruct((POOL_STATE, n, D_MODEL), F32),
        ],
        compiler_params=pltpu.CompilerParams(vmem_limit_bytes=VMEM_LIMIT),
        name="pool_sample",
    )(x, st, g, w, sc)


def _rope_tables(pos):
    half = HEAD_DIM // 2
    inv = ROPE_THETA ** (-np.arange(half, dtype=np.float64) * 2.0 / HEAD_DIM)
    ang = pos.astype(np.float64)[:, None] * inv[None, :]
    cos, sin = np.cos(ang), np.sin(ang)
    return (jnp.asarray(np.concatenate([cos, cos, cos, cos], axis=1), F32),
            jnp.asarray(np.concatenate([-sin, sin, -sin, sin], axis=1), F32))


def _gain_tables(gain):
    half = HEAD_DIM // 2
    swapped = jnp.concatenate([gain[:, half:], gain[:, :half]], axis=1)
    return jnp.concatenate([gain, gain], axis=1), jnp.concatenate([swapped, swapped], axis=1)


def _qkv_kernel(x_ref, g_ref, w_ref, gq_ref, gqs_ref, gk_ref, gks_ref, cos_ref, sin_ref, ones_ref,
                *rest, tm, dils, out_dtype, tail_lens, n_prev, n_cast):
    cast_in = rest[n_prev:n_prev + n_cast]
    rest = rest[n_prev + n_cast:]
    n_out = 3 * N_GROUPS if dils else 1
    outs = rest[:n_out]
    tails = rest[n_out:n_out + len(tail_lens)]
    cast_out = rest[n_out + len(tail_lens):n_out + len(tail_lens) + n_cast]
    stage_ref = rest[n_out + len(tail_lens) + n_cast]
    kv_stage = rest[-1] if tail_lens else None
    for src, dst in zip(cast_in, cast_out):
        dst[...] = src[...].astype(BF16)
    x = x_ref[0]
    h = _rmsnorm(x, g_ref[...]).astype(BF16)
    cosf = cos_ref[...]
    sinf = sin_ref[...]
    lane = lax.broadcasted_iota(jnp.int32, (tm, LANES), 1)
    first_half = (lane % HEAD_DIM) < (HEAD_DIM // 2)
    units = [(part, gi, hh) for part in range(3) for gi in range(N_GROUPS)
             for hh in range(GROUP_W // MXU_DIM)]

    def project(part, gi, hh):
        c0 = (part * N_GROUPS + gi) * GROUP_W + hh * MXU_DIM
        return jnp.dot(h, w_ref[:, c0:c0 + MXU_DIM], preferred_element_type=F32)

    def rope_tables(part, gi):
        ta = (gq_ref, gk_ref)[part][gi:gi + 1, :] * cosf
        tb = (gqs_ref, gks_ref)[part][gi:gi + 1, :] * sinf
        return (ta * Q_SCALE, tb * Q_SCALE) if part == 0 else (ta, tb)

    ahead = [project(*u) for u in units[:QKV_LOOKAHEAD]]
    for idx, (part, gi, hh) in enumerate(units):
        res = ahead.pop(0)
        if idx + QKV_LOOKAHEAD < len(units):
            ahead.append(project(*units[idx + QKV_LOOKAHEAD]))
        slabs = [res[:, sl * LANES:(sl + 1) * LANES] for sl in range(MXU_DIM // LANES)]
        if part < 2:
            if hh == 0:
                ta, tb = rope_tables(part, gi)
            ss = jnp.dot((res * res).astype(BF16), ones_ref[...], preferred_element_type=F32)
            r = lax.rsqrt(ss * (1.0 / HEAD_DIM) + EPS)
            for sl, xs in enumerate(slabs):
                swapped = jnp.where(first_half, pltpu.roll(xs, LANES - HEAD_DIM // 2, 1),
                                    pltpu.roll(xs, HEAD_DIM // 2, 1))
                slabs[sl] = r[:, sl * LANES:(sl + 1) * LANES] * (xs * ta + swapped * tb)
        out_ref = outs[part * N_GROUPS + gi] if dils else None
        d = dils[gi] if dils else 0
        for sl, val in enumerate(slabs):
            c = hh * (MXU_DIM // LANES) + sl
            lanes = slice(c * LANES, (c + 1) * LANES)
            if tail_lens and part > 0:
                t0 = (gi * 2 + part - 1) * GROUP_W
                kv_stage[:, t0 + c * LANES:t0 + (c + 1) * LANES] = val
            if not dils:
                outs[0][part * N_GROUPS + gi, :, lanes] = val.astype(out_dtype)
            elif d == 1:
                out_ref[0, 0, :, lanes] = val.astype(out_dtype)
            else:
                stage_ref[c] = val
                for r_ in range(d):
                    out_ref[0, r_, :, lanes] = stage_ref[c, pl.ds(r_, tm // d, stride=d), :].astype(out_dtype)

    j = pl.program_id(1)
    for gi, n in enumerate(tail_lens):
        rows = min(n, tm)

        @pl.when(j >= pl.num_programs(1) - max(n // tm, 1))
        def _(gi=gi, rows=rows):
            for part in range(2):
                t0 = (gi * 2 + part) * GROUP_W
                tails[gi][0, 0, part * GROUP_W:(part + 1) * GROUP_W, :] = (
                    kv_stage[tm - rows:tm, t0:t0 + GROUP_W].T)


def _qkv(x, g, w, li, gq, gk, cosf, sinf, ones, *, tm, dils, out_dtype, tail_lens=(), prev_tails=None,
         cast=()):
    b, s, _ = x.shape
    nt = s // tm
    gq_t, gq_s = gq
    gk_t, gk_s = gk
    const2 = lambda i, j: (0, 0)
    in_specs = [
        pl.BlockSpec((1, tm, D_MODEL), lambda i, j: (i, j, 0)),
        pl.BlockSpec((1, D_MODEL), const2),
        pl.BlockSpec((D_MODEL, QKV_W), const2),
        pl.BlockSpec((N_GROUPS, LANES), const2),
        pl.BlockSpec((N_GROUPS, LANES), const2),
        pl.BlockSpec((N_GROUPS, LANES), const2),
        pl.BlockSpec((N_GROUPS, LANES), const2),
        pl.BlockSpec((tm, LANES), lambda i, j: (j, 0)),
        pl.BlockSpec((tm, LANES), lambda i, j: (j, 0)),
        pl.BlockSpec((MXU_DIM, MXU_DIM), const2),
    ]
    out_specs, out_shape = [], []
    for _ in range(3):
        for d in dils:
            out_specs.append(pl.BlockSpec((1, d, tm // d, GROUP_W), lambda i, j: (i, 0, j, 0)))
            out_shape.append(jax.ShapeDtypeStruct((b, d, s // d, GROUP_W), out_dtype))
    if not dils:
        assert b == 1 and not tail_lens
        out_specs.append(pl.BlockSpec((3 * N_GROUPS, tm, GROUP_W), lambda i, j: (0, j, 0)))
        out_shape.append(jax.ShapeDtypeStruct((3 * N_GROUPS, s, GROUP_W), out_dtype))
    args = [x, g, w, gq_t, gq_s, gk_t, gk_s, cosf, sinf, ones]
    aliases = {}
    if prev_tails is not None:
        aliases = {len(args) + gi: 9 + gi for gi in range(len(tail_lens))}
        in_specs += [pl.BlockSpec(memory_space=pl.ANY)] * len(tail_lens)
        args += list(prev_tails)
    for n in tail_lens:
        first = nt - max(n // tm, 1)
        out_specs.append(pl.BlockSpec((1, 1, 2 * GROUP_W, min(n, tm)),
                                      lambda i, j, first=first: (li, i, 0, jnp.maximum(j - first, 0))))
        out_shape.append(jax.ShapeDtypeStruct((DEPTH // 2, b, 2 * GROUP_W, n), F32))
    steps = b * nt
    for arr, layer in cast:
        rows = arr.shape[1] // steps
        in_specs.append(pl.BlockSpec((None, rows, arr.shape[2]),
                                     lambda i, j, layer=layer: (layer, i * nt + j, 0)))
        out_specs.append(pl.BlockSpec((rows, arr.shape[2]), lambda i, j: (i * nt + j, 0)))
        out_shape.append(jax.ShapeDtypeStruct(arr.shape[1:], BF16))
        args.append(arr)
    scratch = [pltpu.VMEM((GROUP_W // LANES, tm, LANES), F32)]
    if tail_lens:
        scratch.append(pltpu.VMEM((tm, 2 * N_GROUPS * GROUP_W), F32))
    return pl.pallas_call(
        functools.partial(_qkv_kernel, tm=tm, dils=dils, out_dtype=out_dtype, tail_lens=tuple(tail_lens),
                          n_prev=len(args) - 10 - len(cast), n_cast=len(cast)),
        grid=(b, nt),
        in_specs=in_specs,
        out_specs=out_specs,
        out_shape=out_shape,
        input_output_aliases=aliases,
        scratch_shapes=scratch,
        compiler_params=_params("arbitrary", "arbitrary"),
        name="qkv",
    )(*args)


def _stat_is_max(shape):
    lane = lax.broadcasted_iota(jnp.int32, shape, len(shape) - 1)
    return (lane % HEAD_DIM) < (HEAD_DIM // 2)


def _unpack_stats(packed):
    is_max = _stat_is_max(packed.shape)
    half = HEAD_DIM // 2
    m = jnp.where(is_max, packed, pltpu.roll(packed, half, 1))
    l = jnp.where(is_max, pltpu.roll(packed, LANES - half, 1), packed)
    return m, l


def _attn_kernel(q_ref, kp_ref, kc_ref, vp_ref, vc_ref, o_ref, ml_ref, kcat_ref, vcat_ref, p_scr, m_scr,
                 *, tq, spb):
    j = pl.program_id(1)
    kcat_ref[:, 0:Q_SUB, :] = kp_ref[...]
    kcat_ref[:, Q_SUB:Q_SUB + tq, :] = kc_ref[...]
    vcat_ref[:, 0:Q_SUB, :] = vp_ref[...]
    vcat_ref[:, Q_SUB:Q_SUB + tq, :] = vc_ref[...]

    nk = 2 * Q_SUB
    nq = 2 * Q_SUB
    row = lax.broadcasted_iota(jnp.int32, (nq, nk), 0) & (Q_SUB - 1)
    col = lax.broadcasted_iota(jnp.int32, (nq, nk), 1)
    dist = row + Q_SUB - col
    band = (dist >= 0) & (dist <= N_KEYS_M1)
    lane = lax.broadcasted_iota(jnp.int32, (Q_SUB, LANES), 1)
    low_head = lane < HEAD_DIM
    stat_is_max = _stat_is_max((nq, LANES))
    zero_q = jnp.zeros((Q_SUB, LANES), BF16)
    ones_v = jnp.ones((nk, LANES), BF16)

    first_valid = band & ((col >= Q_SUB) | (j > 0))
    jobs = [(sq, sb * Q_SUB, slice(p * LANES, (p + 1) * LANES))
            for sq in range(spb) for sb in range(tq // Q_SUB) for p in range(GROUP_W // LANES)]

    def probabilities(idx):
        sq, r0, lanes = jobs[idx]
        qp = q_ref[sq, r0:r0 + Q_SUB, lanes]
        q2 = jnp.concatenate([jnp.where(low_head, qp, zero_q), jnp.where(low_head, zero_q, qp)], axis=0)
        s = lax.dot_general(q2, kcat_ref[sq, r0:r0 + nk, lanes], (((1,), (1,)), ((), ())),
                            preferred_element_type=F32)
        s = jnp.where(first_valid if r0 == 0 else band, s, NEG)
        m = jnp.max(s, axis=1, keepdims=True)
        p_scr[idx % ATTN_STAGES] = jnp.exp2(s - m).astype(BF16)
        m_scr[idx % ATTN_STAGES] = jnp.broadcast_to(m, (nq, LANES))

    def outputs(idx):
        sq, r0, lanes = jobs[idx]
        vaug = jnp.concatenate([vcat_ref[sq, r0:r0 + nk, lanes], ones_v], axis=1)
        ov = jnp.dot(p_scr[idx % ATTN_STAGES], vaug, preferred_element_type=F32)
        num = ov[:, :LANES]
        ml = jnp.where(stat_is_max, m_scr[idx % ATTN_STAGES], ov[:, LANES:])
        o_ref[sq, r0:r0 + Q_SUB, lanes] = jnp.where(low_head, num[:Q_SUB], num[Q_SUB:]).astype(BF16)
        ml_ref[sq, r0:r0 + Q_SUB, lanes] = jnp.where(low_head, ml[:Q_SUB], ml[Q_SUB:])

    lag = ATTN_STAGES - 1
    for idx in range(len(jobs) + lag):
        if idx < len(jobs):
            probabilities(idx)
        if idx >= lag:
            outputs(idx - lag)


def _attn(q, k, v, tq):
    nseq, length, _ = q.shape
    ratio = tq // Q_SUB
    spb = max(1, ATTN_TQ // length)
    cur = lambda i, j: (i, j, 0)
    prev = lambda i, j: (i, jnp.maximum(j * ratio - 1, 0), 0)
    return pl.pallas_call(
        functools.partial(_attn_kernel, tq=tq, spb=spb),
        grid=(nseq // spb, length // tq),
        in_specs=[
            pl.BlockSpec((spb, tq, GROUP_W), cur),
            pl.BlockSpec((spb, Q_SUB, GROUP_W), prev),
            pl.BlockSpec((spb, tq, GROUP_W), cur),
            pl.BlockSpec((spb, Q_SUB, GROUP_W), prev),
            pl.BlockSpec((spb, tq, GROUP_W), cur),
        ],
        out_specs=[pl.BlockSpec((spb, tq, GROUP_W), cur), pl.BlockSpec((spb, tq, GROUP_W), cur)],
        out_shape=[jax.ShapeDtypeStruct((nseq, length, GROUP_W), BF16),
                   jax.ShapeDtypeStruct((nseq, length, GROUP_W), F32)],
        scratch_shapes=[pltpu.VMEM((spb, tq + Q_SUB, GROUP_W), BF16)] * 2
        + [pltpu.VMEM((ATTN_STAGES, 2 * Q_SUB, 2 * Q_SUB), BF16),
           pltpu.VMEM((ATTN_STAGES, 2 * Q_SUB, LANES), F32)],
        compiler_params=_params("arbitrary", "arbitrary"),
        name="attn_prompt",
    )(q, k, k, v, v)


def _merge_pieces(load_x, store_y, o_refs, l_refs, wo_ref, und, tm, dils):
    nc = GROUP_W // LANES

    def slabs(ref, d, scr, r0, nr):
        if d == 1:
            return [ref[0, 0, r0:r0 + nr, c * LANES:(c + 1) * LANES] for c in range(nc)]
        for r_ in range(d):
            for c in range(nc):
                scr[c, pl.ds(r0 + r_, nr // d, stride=d), :] = (
                    ref[0, r_, r0 // d:(r0 + nr) // d, c * LANES:(c + 1) * LANES].astype(F32))
        return [scr[c, r0:r0 + nr, :] for c in range(nc)]

    def merge_rows(r0, nr):
        os_ = [slabs(o_refs[g], dils[g], und[2 * g], r0, nr) for g in range(N_GROUPS)]
        ss_ = [slabs(l_refs[g], dils[g], und[2 * g + 1], r0, nr) for g in range(N_GROUPS)]
        merged = []
        for c in range(nc):
            ms, ls_ = zip(*[_unpack_stats(ss_[g][c]) for g in range(N_GROUPS)])
            mx = jnp.maximum(jnp.maximum(ms[0], ms[1]), ms[2])
            ws = [jnp.exp2(m - mx) for m in ms]
            num = ws[0] * os_[0][c] + ws[1] * os_[1][c] + ws[2] * os_[2][c]
            den = ws[0] * ls_[0] + ws[1] * ls_[1] + ws[2] * ls_[2]
            merged.append((num / den).astype(BF16))
        merged = jnp.concatenate(merged, axis=1)
        store_y(r0, nr, load_x(r0, nr) + jnp.dot(merged, wo_ref[...], preferred_element_type=F32))

    nr = min(tm, Q_SUB)
    return [functools.partial(merge_rows, r0, nr) for r0 in range(0, tm, nr)]


def _merge_scratch(tm):
    return [pltpu.VMEM((GROUP_W // LANES, tm, LANES), F32)] * (2 * N_GROUPS)


def _merge_kernel(x_ref, o0, o1, o2, l0, l1, l2, wo_ref, y_ref, *scratch, tm, dils):
    def store_y(r0, nr, v):
        y_ref[0, r0:r0 + nr, :] = v

    for piece in _merge_pieces(lambda r0, nr: x_ref[0, r0:r0 + nr, :], store_y, (o0, o1, o2),
                               (l0, l1, l2), wo_ref, scratch, tm, dils):
        piece()


def _merge_mlp_kernel(x_ref, o0, o1, o2, l0, l1, l2, wo_ref, gmlp_ref, wu_ref, wd_ref, y_ref,
                      *scratch, tm, dils):
    und = scratch[:2 * N_GROUPS]
    mid_next, mid_cur, a_ref = scratch[2 * N_GROUPS:]

    @pl.when(pl.program_id(0) == 0)
    def _():
        mid_cur[...] = jnp.zeros((tm, D_MODEL), F32)

    def store_mid(r0, nr, v):
        mid_next[r0:r0 + nr, :] = v

    side = _merge_pieces(lambda r0, nr: x_ref[r0:r0 + nr, :], store_mid, (o0, o1, o2), (l0, l1, l2),
                         wo_ref, und, tm, dils)
    _mlp_residual(mid_cur[...], gmlp_ref[...], wu_ref, wd_ref, a_ref, y_ref, side, side_offset=1)
    mid_cur[...] = mid_next[...]


def _merge_mlp(x, os_, ls_, wo, li, gmlp, wu, wd, layer, *, tm, dils):
    b, s, _ = x.shape
    tps = s // tm
    n_tiles = b * tps
    x_spec, o_spec = _skew_specs(n_tiles, tm)

    def grp_map(i):
        t = jnp.minimum(i, n_tiles - 1)
        return (t // tps, 0, t % tps, 0)

    grp = [pl.BlockSpec((1, d, tm // d, GROUP_W), grp_map) for d in dils]
    y = pl.pallas_call(
        functools.partial(_merge_mlp_kernel, tm=tm, dils=dils),
        grid=(n_tiles + 1,),
        in_specs=[x_spec] + grp + grp
        + [pl.BlockSpec((None, GROUP_W, D_MODEL), lambda i: (li, 0, 0))] + _mlp_specs(),
        out_specs=o_spec,
        out_shape=jax.ShapeDtypeStruct((b * s, D_MODEL), F32),
        scratch_shapes=_merge_scratch(tm) + _skew_scratch(tm),
        compiler_params=_params("arbitrary"),
        name="merge_mlp",
    )(x.reshape(b * s, D_MODEL), *os_, *ls_, wo, gmlp, wu[layer], wd[layer])
    return y.reshape(b, s, D_MODEL)


def _merge(x, attn, wo, li, *, tm):
    b, s, _ = x.shape
    dils = (1,) * N_GROUPS
    grp = [pl.BlockSpec((None, 1, 1, tm, GROUP_W), lambda i, j, k=k: (k, i, 0, j, 0))
           for k in range(2 * N_GROUPS)]
    return pl.pallas_call(
        functools.partial(_merge_kernel, tm=tm, dils=dils),
        grid=(b, s // tm),
        in_specs=[pl.BlockSpec((1, tm, D_MODEL), lambda i, j: (i, j, 0))] + grp
        + [pl.BlockSpec((None, GROUP_W, D_MODEL), lambda i, j: (li, 0, 0))],
        out_specs=pl.BlockSpec((1, tm, D_MODEL), lambda i, j: (i, j, 0)),
        out_shape=jax.ShapeDtypeStruct((b, s, D_MODEL), F32),
        scratch_shapes=_merge_scratch(tm),
        compiler_params=_params("arbitrary", "arbitrary"),
        name="merge_out_proj",
    )(x, *([attn] * (2 * N_GROUPS)), wo)


def _sample_attend_and_roll(kt, vt, q_row, kn_row, vn_row, new_k, new_v, n, dil):
    hw = kt.shape[0]
    nh = hw // HEAD_DIM
    sub = lax.broadcasted_iota(jnp.int32, (nh, hw), 0)
    ln = lax.broadcasted_iota(jnp.int32, (nh, hw), 1)
    own = (ln // HEAD_DIM) == sub
    e = lax.broadcasted_iota(jnp.int32, (nh, n), 1)
    qb = jnp.where(own, jnp.broadcast_to(q_row, (nh, hw)), 0.0)
    s = jnp.dot(qb.astype(BF16), kt.astype(BF16), preferred_element_type=F32)
    s = jnp.where((n - e) % dil == 0, s, NEG)
    s_new = jnp.sum(qb * kn_row, axis=1, keepdims=True)
    m = jnp.maximum(jnp.max(s, axis=1, keepdims=True), s_new)
    pr = jnp.exp2(s - m)
    p_new = jnp.exp2(s_new - m)
    l = jnp.sum(pr, axis=1, keepdims=True) + p_new
    o = lax.dot_general(pr.astype(BF16), vt.astype(BF16), (((1,), (1,)), ((), ())),
                        preferred_element_type=F32)
    o = o + p_new * vn_row
    o_row = jnp.sum(jnp.where(own, o, 0.0), axis=0, keepdims=True)
    stats = jnp.where(_stat_is_max((nh, hw)), m, l)
    stats_row = jnp.sum(jnp.where(own, stats, 0.0), axis=0, keepdims=True)
    return _shift_in(kt, new_k), _shift_in(vt, new_v), o_row, stats_row


def _shift_in(x, new_col):
    rows, n = x.shape
    lane0 = lax.broadcasted_iota(jnp.int32, (rows, LANES), 1) == 0
    out = []
    for j in range(n // LANES):
        cur = x[:, j * LANES:(j + 1) * LANES]
        nxt = x[:, (j + 1) * LANES:(j + 2) * LANES] if (j + 1) * LANES < n else new_col
        out.append(pltpu.roll(jnp.where(lane0, nxt, cur), LANES - 1, 1))
    return jnp.concatenate(out, axis=1)


def kernel(x_prompt, x_sample, state_pool, cache_kv_w128, cache_kv_w512, cache_kv_w2048, norm_mix,
           norm_mlp, pool_w, pool_scale, attn_w_qkv, attn_q_norm, attn_k_norm, attn_w_o, mlp_w_up,
           mlp_w_down):
    bsz, seq, _ = x_prompt.shape
    nb = x_sample.shape[0]
    past = PAST_LEN
    dils = tuple(d for _, d in ATTN_PATTERNS)

    wu = [mlp_w_up[0].astype(BF16)]
    wd = [mlp_w_down[0].astype(BF16)]
    wqkv = [attn_w_qkv[0].astype(BF16)]
    later = ([(mlp_w_up, l) for l in range(1, DEPTH)] + [(mlp_w_down, l) for l in range(1, DEPTH)]
             + [(attn_w_qkv, l) for l in range(1, DEPTH // 2)])
    wo = attn_w_o.astype(BF16)
    wp = pool_w.astype(BF16)
    head = lax.broadcasted_iota(jnp.int32, (MXU_DIM, MXU_DIM), 0) // HEAD_DIM
    ones = (head == head.T).astype(BF16)
    cos_p, sin_p = _rope_tables(np.arange(seq))
    cos_s, sin_s = _rope_tables(np.full((nb,), past))

    caches = [jnp.transpose(c, (0, 1, 3, 4, 5, 2)).reshape(c.shape[0], nb, 2, GROUP_W, c.shape[2])
              for c in (cache_kv_w128, cache_kv_w512, cache_kv_w2048)]

    xp = x_prompt
    xs = x_sample.reshape(nb, D_MODEL)
    pool_p, pool_s = [], []
    new_caches = kv_tails = None
    hw = GROUP_W // 2
    for layer in range(DEPTH):
        li = layer // 2
        g_mix = norm_mix[layer][None, :]
        g_mlp = norm_mlp[layer][None, :]
        gq = _gain_tables(attn_q_norm[li])
        gk = _gain_tables(attn_k_norm[li])
        if layer % 2 == 0:
            sc = pool_scale[li][None, :]
            st_in = jnp.transpose(state_pool[li], (1, 0, 2))
            xs, st = _pool_sample(xs, st_in, g_mix, wp[li], sc, past)
            pool_s.append(jnp.transpose(st, (1, 0, 2)))
            xs = _mlp(xs, g_mlp, wu, wd, layer, nb)
            packed = _qkv(xs[None], norm_mix[layer + 1][None, :], wqkv[li], li, gq, gk, cos_s, sin_s, ones,
                          tm=nb, dils=(), out_dtype=F32)[0]
            qkn = jnp.transpose(packed.reshape(9, nb, 2, hw), (1, 2, 0, 3))
            kvt = jnp.transpose(packed[N_GROUPS:].reshape(2, N_GROUPS, nb, 2, hw), (3, 1, 0, 4, 2))
            xp, st, new_caches, sample_attn = _pool_mlp(xp, g_mix, wp[li], sc, g_mlp, wu, wd, layer,
                                                        POOL_TILE, caches, li, new_caches, qkn, kvt)
            pool_p.append(st)
        else:
            outs = _qkv(xp, g_mix, wqkv[li], li, gq, gk, cos_p, sin_p, ones, tm=ROW_TILE, dils=dils,
                        out_dtype=BF16, tail_lens=tuple(min(w, seq) for w, _ in ATTN_PATTERNS),
                        prev_tails=kv_tails, cast=later if li == 0 else ())
            kv_tails = outs[9:9 + N_GROUPS]
            if li == 0:
                cast_w = list(outs[9 + N_GROUPS:])
                wu += cast_w[:DEPTH - 1]
                wd += cast_w[DEPTH - 1:2 * (DEPTH - 1)]
                wqkv += cast_w[2 * (DEPTH - 1):]
            os_, ls_ = [], []
            for gi, (w, d) in enumerate(ATTN_PATTERNS):
                q, k, v = (outs[part * N_GROUPS + gi].reshape(bsz * d, seq // d, GROUP_W)
                           for part in range(3))
                o, stats = _attn(q, k, v, min(ATTN_TQ, seq // d))
                os_.append(o.reshape(bsz, d, seq // d, GROUP_W))
                ls_.append(stats.reshape(bsz, d, seq // d, GROUP_W))
            xp = _merge_mlp(xp, os_, ls_, wo, li, g_mlp, wu, wd, layer, tm=ROW_TILE, dils=dils)
            attn = jnp.transpose(sample_attn, (2, 0, 1, 3)).reshape(2 * N_GROUPS, 1, 1, nb, GROUP_W)
            xs = _merge(xs[None], attn, wo, li, tm=nb)[0]
            xs = _mlp(xs, g_mlp, wu, wd, layer, nb)

    kv_s, kv_p = ([jnp.transpose(c.reshape(c.shape[0], c.shape[1], 2, HEADS, HEAD_DIM, c.shape[-1]),
                                 (0, 1, 5, 2, 3, 4)) for c in group] for group in (new_caches, kv_tails))
    return (xp, xs.reshape(nb, 1, D_MODEL), jnp.stack(pool_p), jnp.stack(pool_s),
            kv_p[0], kv_s[0], kv_p[1], kv_s[1], kv_p[2], kv_s[2])
```

```python
import functools

import jax
import jax.numpy as jnp
import numpy as np
from jax import lax
from jax.experimental import pallas as pl
from jax.experimental.pallas import tpu as pltpu

F32 = jnp.float32
BF16 = jnp.bfloat16

D_MODEL = 1024
D_FF = 4 * D_MODEL
DEPTH = 4
POOL_WINDOWS = (2, 4, 8, 16)
POOL_GC = D_MODEL // len(POOL_WINDOWS)
POOL_STATE = max(POOL_WINDOWS) - 1
ATTN_PATTERNS = ((128, 1), (512, 4), (2048, 16))
N_GROUPS = len(ATTN_PATTERNS)
HEAD_DIM = 64
HEADS = 8
GROUP_W = HEADS * HEAD_DIM
QKV_W = 3 * N_GROUPS * GROUP_W
N_KEYS_M1 = 128
ROPE_THETA = 10000.0
PAST_LEN = 8192
EPS = 1e-6
NEG = -1e30
LOG2E = 1.4426950408889634
Q_SCALE = HEAD_DIM ** -0.5 * LOG2E

LANES = 128
MXU_DIM = 256
VMEM_LIMIT = 56 * 1024 * 1024
ROW_TILE = 512
POOL_TILE = 256
FF_CHUNK = 512
DOWN_CHUNK = 256
ATTN_TQ = 2048
Q_SUB = 128
QKV_LOOKAHEAD = 1
ATTN_STAGES = 3


def _params(*sem):
    return pltpu.CompilerParams(dimension_semantics=sem, vmem_limit_bytes=VMEM_LIMIT)


def _rmsnorm(x, g):
    return x * lax.rsqrt(jnp.mean(x * x, axis=-1, keepdims=True) + EPS) * g


def _mlp_residual(x, g, wu_ref, wd_ref, a_ref, o_ref, side=(), side_offset=0):
    h = _rmsnorm(x, g).astype(BF16)
    n_up = D_FF // FF_CHUNK
    n_down = D_MODEL // DOWN_CHUNK
    slots = n_up + n_down - 1

    def run_side(slot):
        for j, piece in enumerate(side):
            if min((2 * j + side_offset) * slots // (2 * len(side)), slots - 1) == slot:
                piece()

    for c in range(n_up):
        cols = slice(c * FF_CHUNK, (c + 1) * FF_CHUNK)
        u = jnp.dot(h, wu_ref[:, cols], preferred_element_type=F32)
        a_ref[:, cols] = jnp.square(jnp.maximum(u, 0.0)).astype(BF16)
        run_side(c)
    for c in range(n_down):
        cols = slice(c * DOWN_CHUNK, (c + 1) * DOWN_CHUNK)
        o_ref[:, cols] = x[:, cols] + jnp.dot(a_ref[...], wd_ref[:, cols], preferred_element_type=F32)
        if c < n_down - 1:
            run_side(n_up + c)


def _mlp_kernel(x_ref, g_ref, wu_ref, wd_ref, o_ref, a_ref):
    _mlp_residual(x_ref[...], g_ref[...], wu_ref, wd_ref, a_ref, o_ref)


def _mlp(x, g, wu, wd, layer, tm):
    n = x.shape[0]
    return pl.pallas_call(
        _mlp_kernel,
        grid=(n // tm,),
        in_specs=[
            pl.BlockSpec((tm, D_MODEL), lambda i: (i, 0)),
            pl.BlockSpec((1, D_MODEL), lambda i: (0, 0)),
            pl.BlockSpec((D_MODEL, D_FF), lambda i: (0, 0)),
            pl.BlockSpec((D_FF, D_MODEL), lambda i: (0, 0)),
        ],
        out_specs=pl.BlockSpec((tm, D_MODEL), lambda i: (i, 0)),
        out_shape=jax.ShapeDtypeStruct((n, D_MODEL), F32),
        scratch_shapes=[pltpu.VMEM((tm, D_FF), BF16)],
        compiler_params=_params("arbitrary"),
        name="mlp",
    )(x, g, wu[layer], wd[layer])


def _skew_specs(n_tiles, tm):
    x_spec = pl.BlockSpec((tm, D_MODEL), lambda i: (jnp.minimum(i, n_tiles - 1), 0))
    o_spec = pl.BlockSpec((tm, D_MODEL), lambda i: (jnp.maximum(i - 1, 0), 0))
    return x_spec, o_spec


def _skew_scratch(tm):
    return [pltpu.VMEM((tm, D_MODEL), F32), pltpu.VMEM((tm, D_MODEL), F32), pltpu.VMEM((tm, D_FF), BF16)]


def _mlp_specs():
    return [pl.BlockSpec((1, D_MODEL), lambda i: (0, 0)),
            pl.BlockSpec((D_MODEL, D_FF), lambda i: (0, 0)),
            pl.BlockSpec((D_FF, D_MODEL), lambda i: (0, 0))]


def _pool_mlp_kernel(x_ref, gmix_ref, w_ref, sc_ref, gmlp_ref, wu_ref, wd_ref, *rest,
                     tm, tiles_per_seq, n_tiles, buf_lens, aliased):
    ng = len(buf_lens)
    c_refs, (qkn_ref, kvt_ref) = rest[:ng], rest[ng:ng + 2]
    outs = rest[ng + 2 + (ng if aliased else 0):]
    o_ref, st_ref = outs[:2]
    co_refs, ol_ref = outs[2:2 + ng], outs[2 + ng]
    ext_ref, mid_next, mid_cur, a_ref = outs[3 + ng:]
    i = pl.program_id(0)
    halo = POOL_STATE + 1

    @pl.when(i == 0)
    def _():
        ext_ref[0:halo, :] = jnp.zeros((halo, D_MODEL), F32)
        mid_cur[...] = jnp.zeros((tm, D_MODEL), F32)

    s = jnp.minimum(i, n_tiles - 1) % tiles_per_seq
    ext_ref[0:halo, :] = jnp.where(s == 0, 0.0, ext_ref[0:halo, :])

    def normalise():
        ext_ref[halo:halo + tm, :] = _rmsnorm(x_ref[...], gmix_ref[...])

    def pool_block(gi, r0, nr):
        w = POOL_WINDOWS[gi]
        cols = slice(gi * POOL_GC, (gi + 1) * POOL_GC)
        pos = s * tm + r0 + lax.broadcasted_iota(jnp.int32, (nr, 1), 0)
        cur = ext_ref[halo + r0:halo + r0 + nr, cols]
        acc = cur
        for k in range(1, w):
            acc = acc + ext_ref[halo + r0 - k:halo + r0 - k + nr, cols]
        cnt = jnp.minimum(pos + 1, w).astype(F32)
        d = (acc / cnt - cur).astype(BF16)
        y = jnp.dot(d, w_ref[gi], preferred_element_type=F32)
        mid_next[r0:r0 + nr, cols] = x_ref[r0:r0 + nr, cols] + y * sc_ref[:, cols]

    t = jnp.minimum(i, n_tiles - 1)
    sample, half = t // 2, t % 2

    def buffer_group(g):
        kt, vt = c_refs[g][0, 0, 0], c_refs[g][0, 0, 1]
        lane = lax.broadcasted_iota(jnp.int32, kvt_ref.shape[3:], 1)
        new_k = jnp.sum(jnp.where(lane == sample, kvt_ref[half, g, 0], 0.0), axis=1, keepdims=True)
        new_v = jnp.sum(jnp.where(lane == sample, kvt_ref[half, g, 1], 0.0), axis=1, keepdims=True)
        rk, rv, o_row, lse_row = _sample_attend_and_roll(
            kt, vt, qkn_ref[0, 0, g:g + 1, :], qkn_ref[0, 0, ng + g:ng + g + 1, :],
            qkn_ref[0, 0, 2 * ng + g:2 * ng + g + 1, :], new_k, new_v, buf_lens[g], ATTN_PATTERNS[g][1])
        co_refs[g][0, 0, 0] = rk
        co_refs[g][0, 0, 1] = rv
        ol_ref[0, 0, g:g + 1, :] = o_row
        ol_ref[0, 0, ng + g:ng + g + 1, :] = lse_row

    side = [normalise]
    for gi, splits in ((3, 4), (2, 2), (1, 1), (0, 1)):
        nr = tm // splits
        side += [functools.partial(pool_block, gi, p * nr, nr) for p in range(splits)]
    for g in range(ng):
        side.insert(1 + 3 * g, functools.partial(buffer_group, ng - 1 - g))
    _mlp_residual(mid_cur[...], gmlp_ref[...], wu_ref, wd_ref, a_ref, o_ref, side)

    st_ref[0] = ext_ref[tm + 1:tm + halo, :]
    ext_ref[0:halo, :] = ext_ref[tm:tm + halo, :]
    mid_cur[...] = mid_next[...]


def _pool_mlp(x, gmix, w, sc, gmlp, wu, wd, layer, tm, caches, li, prev_rolled, qkn, kvt):
    b, s, _ = x.shape
    tps = s // tm
    n_tiles = b * tps
    nb = qkn.shape[0]
    hw = GROUP_W // 2
    assert n_tiles == 2 * nb, "one half of one sample's heads rides along with every row tile"
    ng = len(caches)
    buf_lens = tuple(c.shape[-1] for c in caches)
    aliased = prev_rolled is not None
    x_spec, o_spec = _skew_specs(n_tiles, tm)
    const = lambda i: (0, 0)

    def half_sample(i):
        t = jnp.minimum(i, n_tiles - 1)
        return t // 2, t % 2

    def cache_map(i):
        smp, half = half_sample(i)
        return (li, smp, 0, half, 0)

    def row_map(i):
        smp, half = half_sample(i)
        return (smp, half, 0, 0)

    cblk = [pl.BlockSpec((1, 1, 2, hw, n), cache_map) for n in buf_lens]
    in_specs = [
        x_spec,
        pl.BlockSpec((1, D_MODEL), const),
        pl.BlockSpec((len(POOL_WINDOWS), POOL_GC, POOL_GC), lambda i: (0, 0, 0)),
        pl.BlockSpec((1, D_MODEL), const),
    ] + _mlp_specs() + cblk + [
        pl.BlockSpec((1, 1, 3 * ng, hw), row_map),
        pl.BlockSpec(kvt.shape, lambda i: (0, 0, 0, 0, 0)),
    ]
    args = [x.reshape(b * s, D_MODEL), gmix, w, sc, gmlp, wu[layer], wd[layer], *caches, qkn, kvt]
    aliases = {}
    if aliased:
        aliases = {len(args) + g: 2 + g for g in range(ng)}
        in_specs += [pl.BlockSpec(memory_space=pl.ANY)] * ng
        args += list(prev_rolled)
    res = pl.pallas_call(
        functools.partial(_pool_mlp_kernel, tm=tm, tiles_per_seq=tps, n_tiles=n_tiles,
                          buf_lens=buf_lens, aliased=aliased),
        grid=(n_tiles + 1,),
        in_specs=in_specs,
        out_specs=[
            o_spec,
            pl.BlockSpec((1, POOL_STATE, D_MODEL),
                         lambda i: (jnp.minimum(i, n_tiles - 1) // tps, 0, 0)),
        ] + cblk + [pl.BlockSpec((1, 1, 2 * ng, hw), row_map)],
        out_shape=[
            jax.ShapeDtypeStruct((b * s, D_MODEL), F32),
            jax.ShapeDtypeStruct((b, POOL_STATE, D_MODEL), F32),
        ] + [jax.ShapeDtypeStruct(c.shape, F32) for c in caches]
        + [jax.ShapeDtypeStruct((nb, 2, 2 * ng, hw), F32)],
        input_output_aliases=aliases,
        scratch_shapes=[pltpu.VMEM((tm + POOL_STATE + 1, D_MODEL), F32)] + _skew_scratch(tm),
        compiler_params=_params("arbitrary"),
        name="pool_mlp",
    )(*args)
    return res[0].reshape(b, s, D_MODEL), res[1], list(res[2:2 + ng]), res[2 + ng]


def _pool_sample_kernel(x_ref, st_ref, g_ref, w_ref, sc_ref, o_ref, nst_ref, *, pos):
    x = x_ref[...]
    xn = _rmsnorm(x, g_ref[...])
    for gi, w in enumerate(POOL_WINDOWS):
        cols = slice(gi * POOL_GC, (gi + 1) * POOL_GC)
        cur = xn[:, cols]
        acc = cur
        for k in range(1, w):
            acc = acc + st_ref[POOL_STATE - k, :, cols]
        d = acc / float(min(pos + 1, w)) - cur
        y = jnp.dot(d.astype(BF16), w_ref[gi], preferred_element_type=F32)
        o_ref[:, cols] = x[:, cols] + y * sc_ref[:, cols]
    for k in range(POOL_STATE - 1):
        nst_ref[k] = st_ref[k + 1]
    nst_ref[POOL_STATE - 1] = xn


def _pool_sample(x, st, g, w, sc, pos):
    n = x.shape[0]
    return pl.pallas_call(
        functools.partial(_pool_sample_kernel, pos=pos),
        out_shape=[
            jax.ShapeDtypeStruct((n, D_MODEL), F32),
            jax.ShapeDtypeStruct((POOL_STATE, n, D_MODEL), F32),
        ],
        compiler_params=pltpu.CompilerParams(vmem_limit_bytes=VMEM_LIMIT),
        name="pool_sample",
    )(x, st, g, w, sc)


def _rope_tables(pos):
    half = HEAD_DIM // 2
    inv = ROPE_THETA ** (-np.arange(half, dtype=np.float64) * 2.0 / HEAD_DIM)
    ang = pos.astype(np.float64)[:, None] * inv[None, :]
    cos, sin = np.cos(ang), np.sin(ang)
    return (jnp.asarray(np.concatenate([cos, cos, cos, cos], axis=1), F32),
            jnp.asarray(np.concatenate([-sin, sin, -sin, sin], axis=1), F32))


def _gain_tables(gain):
    half = HEAD_DIM // 2
    swapped = jnp.concatenate([gain[:, half:], gain[:, :half]], axis=1)
    return jnp.concatenate([gain, gain], axis=1), jnp.concatenate([swapped, swapped], axis=1)


def _qkv_kernel(x_ref, g_ref, w_ref, gq_ref, gqs_ref, gk_ref, gks_ref, cos_ref, sin_ref, ones_ref,
                *rest, tm, dils, out_dtype, tail_lens, n_prev, n_cast):
    cast_in = rest[n_prev:n_prev + n_cast]
    rest = rest[n_prev + n_cast:]
    n_out = 3 * N_GROUPS if dils else 1
    outs = rest[:n_out]
    tails = rest[n_out:n_out + len(tail_lens)]
    cast_out = rest[n_out + len(tail_lens):n_out + len(tail_lens) + n_cast]
    stage_ref = rest[n_out + len(tail_lens) + n_cast]
    kv_stage = rest[-1] if tail_lens else None
    for src, dst in zip(cast_in, cast_out):
        dst[...] = src[...].astype(BF16)
    x = x_ref[0]
    h = _rmsnorm(x, g_ref[...]).astype(BF16)
    cosf = cos_ref[...]
    sinf = sin_ref[...]
    lane = lax.broadcasted_iota(jnp.int32, (tm, LANES), 1)
    first_half = (lane % HEAD_DIM) < (HEAD_DIM // 2)
    units = [(part, gi, hh) for part in range(3) for gi in range(N_GROUPS)
             for hh in range(GROUP_W // MXU_DIM)]

    def project(part, gi, hh):
        c0 = (part * N_GROUPS + gi) * GROUP_W + hh * MXU_DIM
        return jnp.dot(h, w_ref[:, c0:c0 + MXU_DIM], preferred_element_type=F32)

    def rope_tables(part, gi):
        ta = (gq_ref, gk_ref)[part][gi:gi + 1, :] * cosf
        tb = (gqs_ref, gks_ref)[part][gi:gi + 1, :] * sinf
        return (ta * Q_SCALE, tb * Q_SCALE) if part == 0 else (ta, tb)

    ahead = [project(*u) for u in units[:QKV_LOOKAHEAD]]
    for idx, (part, gi, hh) in enumerate(units):
        res = ahead.pop(0)
        if idx + QKV_LOOKAHEAD < len(units):
            ahead.append(project(*units[idx + QKV_LOOKAHEAD]))
        slabs = [res[:, sl * LANES:(sl + 1) * LANES] for sl in range(MXU_DIM // LANES)]
        if part < 2:
            if hh == 0:
                ta, tb = rope_tables(part, gi)
            ss = jnp.dot((res * res).astype(BF16), ones_ref[...], preferred_element_type=F32)
            r = lax.rsqrt(ss * (1.0 / HEAD_DIM) + EPS)
            for sl, xs in enumerate(slabs):
                swapped = jnp.where(first_half, pltpu.roll(xs, LANES - HEAD_DIM // 2, 1),
                                    pltpu.roll(xs, HEAD_DIM // 2, 1))
                slabs[sl] = r[:, sl * LANES:(sl + 1) * LANES] * (xs * ta + swapped * tb)
        out_ref = outs[part * N_GROUPS + gi] if dils else None
        d = dils[gi] if dils else 0
        for sl, val in enumerate(slabs):
            c = hh * (MXU_DIM // LANES) + sl
            lanes = slice(c * LANES, (c + 1) * LANES)
            if tail_lens and part > 0:
                t0 = (gi * 2 + part - 1) * GROUP_W
                kv_stage[:, t0 + c * LANES:t0 + (c + 1) * LANES] = val
            if not dils:
                outs[0][part * N_GROUPS + gi, :, lanes] = val.astype(out_dtype)
            elif d == 1:
                out_ref[0, 0, :, lanes] = val.astype(out_dtype)
            else:
                stage_ref[c] = val
                for r_ in range(d):
                    out_ref[0, r_, :, lanes] = stage_ref[c, pl.ds(r_, tm // d, stride=d), :].astype(out_dtype)

    j = pl.program_id(1)
    for gi, n in enumerate(tail_lens):
        rows = min(n, tm)

        @pl.when(j >= pl.num_programs(1) - max(n // tm, 1))
        def _(gi=gi, rows=rows):
            for part in range(2):
                t0 = (gi * 2 + part) * GROUP_W
                tails[gi][0, 0, part * GROUP_W:(part + 1) * GROUP_W, :] = (
                    kv_stage[tm - rows:tm, t0:t0 + GROUP_W].T)


def _qkv(x, g, w, li, gq, gk, cosf, sinf, ones, *, tm, dils, out_dtype, tail_lens=(), prev_tails=None,
         cast=()):
    b, s, _ = x.shape
    nt = s // tm
    gq_t, gq_s = gq
    gk_t, gk_s = gk
    const2 = lambda i, j: (0, 0)
    in_specs = [
        pl.BlockSpec((1, tm, D_MODEL), lambda i, j: (i, j, 0)),
        pl.BlockSpec((1, D_MODEL), const2),
        pl.BlockSpec((D_MODEL, QKV_W), const2),
        pl.BlockSpec((N_GROUPS, LANES), const2),
        pl.BlockSpec((N_GROUPS, LANES), const2),
        pl.BlockSpec((N_GROUPS, LANES), const2),
        pl.BlockSpec((N_GROUPS, LANES), const2),
        pl.BlockSpec((tm, LANES), lambda i, j: (j, 0)),
        pl.BlockSpec((tm, LANES), lambda i, j: (j, 0)),
        pl.BlockSpec((MXU_DIM, MXU_DIM), const2),
    ]
    out_specs, out_shape = [], []
    for _ in range(3):
        for d in dils:
            out_specs.append(pl.BlockSpec((1, d, tm // d, GROUP_W), lambda i, j: (i, 0, j, 0)))
            out_shape.append(jax.ShapeDtypeStruct((b, d, s // d, GROUP_W), out_dtype))
    if not dils:
        assert b == 1 and not tail_lens
        out_specs.append(pl.BlockSpec((3 * N_GROUPS, tm, GROUP_W), lambda i, j: (0, j, 0)))
        out_shape.append(jax.ShapeDtypeStruct((3 * N_GROUPS, s, GROUP_W), out_dtype))
    args = [x, g, w, gq_t, gq_s, gk_t, gk_s, cosf, sinf, ones]
    aliases = {}
    if prev_tails is not None:
        aliases = {len(args) + gi: 9 + gi for gi in range(len(tail_lens))}
        in_specs += [pl.BlockSpec(memory_space=pl.ANY)] * len(tail_lens)
        args += list(prev_tails)
    for n in tail_lens:
        first = nt - max(n // tm, 1)
        out_specs.append(pl.BlockSpec((1, 1, 2 * GROUP_W, min(n, tm)),
                                      lambda i, j, first=first: (li, i, 0, jnp.maximum(j - first, 0))))
        out_shape.append(jax.ShapeDtypeStruct((DEPTH // 2, b, 2 * GROUP_W, n), F32))
    steps = b * nt
    for arr, layer in cast:
        rows = arr.shape[1] // steps
        in_specs.append(pl.BlockSpec((None, rows, arr.shape[2]),
                                     lambda i, j, layer=layer: (layer, i * nt + j, 0)))
        out_specs.append(pl.BlockSpec((rows, arr.shape[2]), lambda i, j: (i * nt + j, 0)))
        out_shape.append(jax.ShapeDtypeStruct(arr.shape[1:], BF16))
        args.append(arr)
    scratch = [pltpu.VMEM((GROUP_W // LANES, tm, LANES), F32)]
    if tail_lens:
        scratch.append(pltpu.VMEM((tm, 2 * N_GROUPS * GROUP_W), F32))
    return pl.pallas_call(
        functools.partial(_qkv_kernel, tm=tm, dils=dils, out_dtype=out_dtype, tail_lens=tuple(tail_lens),
                          n_prev=len(args) - 10 - len(cast), n_cast=len(cast)),
        grid=(b, nt),
        in_specs=in_specs,
        out_specs=out_specs,
        out_shape=out_shape,
        input_output_aliases=aliases,
        scratch_shapes=scratch,
        compiler_params=_params("arbitrary", "arbitrary"),
        name="qkv",
    )(*args)


def _stat_is_max(shape):
    lane = lax.broadcasted_iota(jnp.int32, shape, len(shape) - 1)
    return (lane % HEAD_DIM) < (HEAD_DIM // 2)


def _unpack_stats(packed):
    is_max = _stat_is_max(packed.shape)
    half = HEAD_DIM // 2
    m = jnp.where(is_max, packed, pltpu.roll(packed, half, 1))
    l = jnp.where(is_max, pltpu.roll(packed, LANES - half, 1), packed)
    return m, l


def _attn_kernel(q_ref, kp_ref, kc_ref, vp_ref, vc_ref, o_ref, ml_ref, kcat_ref, vcat_ref, p_scr, m_scr,
                 *, tq, spb):
    j = pl.program_id(1)
    kcat_ref[:, 0:Q_SUB, :] = kp_ref[...]
    kcat_ref[:, Q_SUB:Q_SUB + tq, :] = kc_ref[...]
    vcat_ref[:, 0:Q_SUB, :] = vp_ref[...]
    vcat_ref[:, Q_SUB:Q_SUB + tq, :] = vc_ref[...]

    nk = 2 * Q_SUB
    nq = 2 * Q_SUB
    row = lax.broadcasted_iota(jnp.int32, (nq, nk), 0) & (Q_SUB - 1)
    col = lax.broadcasted_iota(jnp.int32, (nq, nk), 1)
    dist = row + Q_SUB - col
    band = (dist >= 0) & (dist <= N_KEYS_M1)
    lane = lax.broadcasted_iota(jnp.int32, (Q_SUB, LANES), 1)
    low_head = lane < HEAD_DIM
    stat_is_max = _stat_is_max((nq, LANES))
    zero_q = jnp.zeros((Q_SUB, LANES), BF16)
    ones_v = jnp.ones((nk, LANES), BF16)

    first_valid = band & ((col >= Q_SUB) | (j > 0))
    jobs = [(sq, sb * Q_SUB, slice(p * LANES, (p + 1) * LANES))
            for sq in range(spb) for sb in range(tq // Q_SUB) for p in range(GROUP_W // LANES)]

    def probabilities(idx):
        sq, r0, lanes = jobs[idx]
        qp = q_ref[sq, r0:r0 + Q_SUB, lanes]
        q2 = jnp.concatenate([jnp.where(low_head, qp, zero_q), jnp.where(low_head, zero_q, qp)], axis=0)
        s = lax.dot_general(q2, kcat_ref[sq, r0:r0 + nk, lanes], (((1,), (1,)), ((), ())),
                            preferred_element_type=F32)
        s = jnp.where(first_valid if r0 == 0 else band, s, NEG)
        m = jnp.max(s, axis=1, keepdims=True)
        p_scr[idx % ATTN_STAGES] = jnp.exp2(s - m).astype(BF16)
        m_scr[idx % ATTN_STAGES] = jnp.broadcast_to(m, (nq, LANES))

    def outputs(idx):
        sq, r0, lanes = jobs[idx]
        vaug = jnp.concatenate([vcat_ref[sq, r0:r0 + nk, lanes], ones_v], axis=1)
        ov = jnp.dot(p_scr[idx % ATTN_STAGES], vaug, preferred_element_type=F32)
        num = ov[:, :LANES]
        ml = jnp.where(stat_is_max, m_scr[idx % ATTN_STAGES], ov[:, LANES:])
        o_ref[sq, r0:r0 + Q_SUB, lanes] = jnp.where(low_head, num[:Q_SUB], num[Q_SUB:]).astype(BF16)
        ml_ref[sq, r0:r0 + Q_SUB, lanes] = jnp.where(low_head, ml[:Q_SUB], ml[Q_SUB:])

    lag = ATTN_STAGES - 1
    for idx in range(len(jobs) + lag):
        if idx < len(jobs):
            probabilities(idx)
        if idx >= lag:
            outputs(idx - lag)


def _attn(q, k, v, tq):
    nseq, length, _ = q.shape
    ratio = tq // Q_SUB
    spb = max(1, ATTN_TQ // length)
    cur = lambda i, j: (i, j, 0)
    prev = lambda i, j: (i, jnp.maximum(j * ratio - 1, 0), 0)
    return pl.pallas_call(
        functools.partial(_attn_kernel, tq=tq, spb=spb),
        grid=(nseq // spb, length // tq),
        in_specs=[
            pl.BlockSpec((spb, tq, GROUP_W), cur),
            pl.BlockSpec((spb, Q_SUB, GROUP_W), prev),
            pl.BlockSpec((spb, tq, GROUP_W), cur),
            pl.BlockSpec((spb, Q_SUB, GROUP_W), prev),
            pl.BlockSpec((spb, tq, GROUP_W), cur),
        ],
        out_specs=[pl.BlockSpec((spb, tq, GROUP_W), cur), pl.BlockSpec((spb, tq, GROUP_W), cur)],
        out_shape=[jax.ShapeDtypeStruct((nseq, length, GROUP_W), BF16),
                   jax.ShapeDtypeStruct((nseq, length, GROUP_W), F32)],
        scratch_shapes=[pltpu.VMEM((spb, tq + Q_SUB, GROUP_W), BF16)] * 2
        + [pltpu.VMEM((ATTN_STAGES, 2 * Q_SUB, 2 * Q_SUB), BF16),
           pltpu.VMEM((ATTN_STAGES, 2 * Q_SUB, LANES), F32)],
        compiler_params=_params("arbitrary", "arbitrary"),
        name="attn_prompt",
    )(q, k, k, v, v)


def _merge_pieces(load_x, store_y, o_refs, l_refs, wo_ref, und, tm, dils):
    nc = GROUP_W // LANES

    def slabs(ref, d, scr, r0, nr):
        if d == 1:
            return [ref[0, 0, r0:r0 + nr, c * LANES:(c + 1) * LANES] for c in range(nc)]
        for r_ in range(d):
            for c in range(nc):
                scr[c, pl.ds(r0 + r_, nr // d, stride=d), :] = (
                    ref[0, r_, r0 // d:(r0 + nr) // d, c * LANES:(c + 1) * LANES].astype(F32))
        return [scr[c, r0:r0 + nr, :] for c in range(nc)]

    def merge_rows(r0, nr):
        os_ = [slabs(o_refs[g], dils[g], und[2 * g], r0, nr) for g in range(N_GROUPS)]
        ss_ = [slabs(l_refs[g], dils[g], und[2 * g + 1], r0, nr) for g in range(N_GROUPS)]
        merged = []
        for c in range(nc):
            ms, ls_ = zip(*[_unpack_stats(ss_[g][c]) for g in range(N_GROUPS)])
            mx = jnp.maximum(jnp.maximum(ms[0], ms[1]), ms[2])
            ws = [jnp.exp2(m - mx) for m in ms]
            num = ws[0] * os_[0][c] + ws[1] * os_[1][c] + ws[2] * os_[2][c]
            den = ws[0] * ls_[0] + ws[1] * ls_[1] + ws[2] * ls_[2]
            merged.append((num / den).astype(BF16))
        merged = jnp.concatenate(merged, axis=1)
        store_y(r0, nr, load_x(r0, nr) + jnp.dot(merged, wo_ref[...], preferred_element_type=F32))

    nr = min(tm, Q_SUB)
    return [functools.partial(merge_rows, r0, nr) for r0 in range(0, tm, nr)]


def _merge_scratch(tm):
    return [pltpu.VMEM((GROUP_W // LANES, tm, LANES), F32)] * (2 * N_GROUPS)


def _merge_kernel(x_ref, o0, o1, o2, l0, l1, l2, wo_ref, y_ref, *scratch, tm, dils):
    def store_y(r0, nr, v):
        y_ref[0, r0:r0 + nr, :] = v

    for piece in _merge_pieces(lambda r0, nr: x_ref[0, r0:r0 + nr, :], store_y, (o0, o1, o2),
                               (l0, l1, l2), wo_ref, scratch, tm, dils):
        piece()


def _merge_mlp_kernel(x_ref, o0, o1, o2, l0, l1, l2, wo_ref, gmlp_ref, wu_ref, wd_ref, y_ref,
                      *scratch, tm, dils):
    und = scratch[:2 * N_GROUPS]
    mid_next, mid_cur, a_ref = scratch[2 * N_GROUPS:]

    @pl.when(pl.program_id(0) == 0)
    def _():
        mid_cur[...] = jnp.zeros((tm, D_MODEL), F32)

    def store_mid(r0, nr, v):
        mid_next[r0:r0 + nr, :] = v

    side = _merge_pieces(lambda r0, nr: x_ref[r0:r0 + nr, :], store_mid, (o0, o1, o2), (l0, l1, l2),
                         wo_ref, und, tm, dils)
    _mlp_residual(mid_cur[...], gmlp_ref[...], wu_ref, wd_ref, a_ref, y_ref, side, side_offset=1)
    mid_cur[...] = mid_next[...]


def _merge_mlp(x, os_, ls_, wo, li, gmlp, wu, wd, layer, *, tm, dils):
    b, s, _ = x.shape
    tps = s // tm
    n_tiles = b * tps
    x_spec, o_spec = _skew_specs(n_tiles, tm)

    def grp_map(i):
        t = jnp.minimum(i, n_tiles - 1)
        return (t // tps, 0, t % tps, 0)

    grp = [pl.BlockSpec((1, d, tm // d, GROUP_W), grp_map) for d in dils]
    y = pl.pallas_call(
        functools.partial(_merge_mlp_kernel, tm=tm, dils=dils),
        grid=(n_tiles + 1,),
        in_specs=[x_spec] + grp + grp
        + [pl.BlockSpec((None, GROUP_W, D_MODEL), lambda i: (li, 0, 0))] + _mlp_specs(),
        out_specs=o_spec,
        out_shape=jax.ShapeDtypeStruct((b * s, D_MODEL), F32),
        scratch_shapes=_merge_scratch(tm) + _skew_scratch(tm),
        compiler_params=_params("arbitrary"),
        name="merge_mlp",
    )(x.reshape(b * s, D_MODEL), *os_, *ls_, wo, gmlp, wu[layer], wd[layer])
    return y.reshape(b, s, D_MODEL)


def _merge(x, attn, wo, li, *, tm):
    b, s, _ = x.shape
    dils = (1,) * N_GROUPS
    grp = [pl.BlockSpec((None, 1, 1, tm, GROUP_W), lambda i, j, k=k: (k, i, 0, j, 0))
           for k in range(2 * N_GROUPS)]
    return pl.pallas_call(
        functools.partial(_merge_kernel, tm=tm, dils=dils),
        grid=(b, s // tm),
        in_specs=[pl.BlockSpec((1, tm, D_MODEL), lambda i, j: (i, j, 0))] + grp
        + [pl.BlockSpec((None, GROUP_W, D_MODEL), lambda i, j: (li, 0, 0))],
        out_specs=pl.BlockSpec((1, tm, D_MODEL), lambda i, j: (i, j, 0)),
        out_shape=jax.ShapeDtypeStruct((b, s, D_MODEL), F32),
        scratch_shapes=_merge_scratch(tm),
        compiler_params=_params("arbitrary", "arbitrary"),
        name="merge_out_proj",
    )(x, *([attn] * (2 * N_GROUPS)), wo)


def _sample_attend_and_roll(kt, vt, q_row, kn_row, vn_row, new_k, new_v, n, dil):
    hw = kt.shape[0]
    nh = hw // HEAD_DIM
    sub = lax.broadcasted_iota(jnp.int32, (nh, hw), 0)
    ln = lax.broadcasted_iota(jnp.int32, (nh, hw), 1)
    own = (ln // HEAD_DIM) == sub
    e = lax.broadcasted_iota(jnp.int32, (nh, n), 1)
    qb = jnp.where(own, jnp.broadcast_to(q_row, (nh, hw)), 0.0)
    s = jnp.dot(qb.astype(BF16), kt.astype(BF16), preferred_element_type=F32)
    s = jnp.where((n - e) % dil == 0, s, NEG)
    s_new = jnp.sum(qb * kn_row, axis=1, keepdims=True)
    m = jnp.maximum(jnp.max(s, axis=1, keepdims=True), s_new)
    pr = jnp.exp2(s - m)
    p_new = jnp.exp2(s_new - m)
    l = jnp.sum(pr, axis=1, keepdims=True) + p_new
    o = lax.dot_general(pr.astype(BF16), vt.astype(BF16), (((1,), (1,)), ((), ())),
                        preferred_element_type=F32)
    o = o + p_new * vn_row
    o_row = jnp.sum(jnp.where(own, o, 0.0), axis=0, keepdims=True)
    stats = jnp.where(_stat_is_max((nh, hw)), m, l)
    stats_row = jnp.sum(jnp.where(own, stats, 0.0), axis=0, keepdims=True)
    return _shift_in(kt, new_k), _shift_in(vt, new_v), o_row, stats_row


def _shift_in(x, new_col):
    rows, n = x.shape
    lane0 = lax.broadcasted_iota(jnp.int32, (rows, LANES), 1) == 0
    out = []
    for j in range(n // LANES):
        cur = x[:, j * LANES:(j + 1) * LANES]
        nxt = x[:, (j + 1) * LANES:(j + 2) * LANES] if (j + 1) * LANES < n else new_col
        out.append(pltpu.roll(jnp.where(lane0, nxt, cur), LANES - 1, 1))
    return jnp.concatenate(out, axis=1)


def kernel(x_prompt, x_sample, state_pool, cache_kv_w128, cache_kv_w512, cache_kv_w2048, norm_mix,
           norm_mlp, pool_w, pool_scale, attn_w_qkv, attn_q_norm, attn_k_norm, attn_w_o, mlp_w_up,
           mlp_w_down):
    bsz, seq, _ = x_prompt.shape
    nb = x_sample.shape[0]
    past = PAST_LEN
    dils = tuple(d for _, d in ATTN_PATTERNS)

    wu = [mlp_w_up[0].astype(BF16)]
    wd = [mlp_w_down[0].astype(BF16)]
    wqkv = [attn_w_qkv[0].astype(BF16)]
    later = ([(mlp_w_up, l) for l in range(1, DEPTH)] + [(mlp_w_down, l) for l in range(1, DEPTH)]
             + [(attn_w_qkv, l) for l in range(1, DEPTH // 2)])
    wo = attn_w_o.astype(BF16)
    wp = pool_w.astype(BF16)
    head = lax.broadcasted_iota(jnp.int32, (MXU_DIM, MXU_DIM), 0) // HEAD_DIM
    ones = (head == head.T).astype(BF16)
    cos_p, sin_p = _rope_tables(np.arange(seq))
    cos_s, sin_s = _rope_tables(np.full((nb,), past))

    caches = [jnp.transpose(c, (0, 1, 3, 4, 5, 2)).reshape(c.shape[0], nb, 2, GROUP_W, c.shape[2])
              for c in (cache_kv_w128, cache_kv_w512, cache_kv_w2048)]

    xp = x_prompt
    xs = x_sample.reshape(nb, D_MODEL)
    pool_p, pool_s = [], []
    new_caches = kv_tails = None
    hw = GROUP_W // 2
    for layer in range(DEPTH):
        li = layer // 2
        g_mix = norm_mix[layer][None, :]
        g_mlp = norm_mlp[layer][None, :]
        gq = _gain_tables(attn_q_norm[li])
        gk = _gain_tables(attn_k_norm[li])
        if layer % 2 == 0:
            sc = pool_scale[li][None, :]
            st_in = jnp.transpose(state_pool[li], (1, 0, 2))
            xs, st = _pool_sample(xs, st_in, g_mix, wp[li], sc, past)
            pool_s.append(jnp.transpose(st, (1, 0, 2)))
            xs = _mlp(xs, g_mlp, wu, wd, layer, nb)
            packed = _qkv(xs[None], norm_mix[layer + 1][None, :], wqkv[li], li, gq, gk, cos_s, sin_s, ones,
                          tm=nb, dils=(), out_dtype=F32)[0]
            qkn = jnp.transpose(packed.reshape(9, nb, 2, hw), (1, 2, 0, 3))
            kvt = jnp.transpose(packed[N_GROUPS:].reshape(2, N_GROUPS, nb, 2, hw), (3, 1, 0, 4, 2))
            xp, st, new_caches, sample_attn = _pool_mlp(xp, g_mix, wp[li], sc, g_mlp, wu, wd, layer,
                                                        POOL_TILE, caches, li, new_caches, qkn, kvt)
            pool_p.append(st)
        else:
            outs = _qkv(xp, g_mix, wqkv[li], li, gq, gk, cos_p, sin_p, ones, tm=ROW_TILE, dils=dils,
                        out_dtype=BF16, tail_lens=tuple(min(w, seq) for w, _ in ATTN_PATTERNS),
                        prev_tails=kv_tails, cast=later if li == 0 else ())
            kv_tails = outs[9:9 + N_GROUPS]
            if li == 0:
                cast_w = list(outs[9 + N_GROUPS:])
                wu += cast_w[:DEPTH - 1]
                wd += cast_w[DEPTH - 1:2 * (DEPTH - 1)]
                wqkv += cast_w[2 * (DEPTH - 1):]
            os_, ls_ = [], []
            for gi, (w, d) in enumerate(ATTN_PATTERNS):
                q, k, v = (outs[part * N_GROUPS + gi].reshape(bsz * d, seq // d, GROUP_W)
                           for part in range(3))
                o, stats = _attn(q, k, v, min(ATTN_TQ, seq // d))
                os_.append(o.reshape(bsz, d, seq // d, GROUP_W))
                ls_.append(stats.reshape(bsz, d, seq // d, GROUP_W))
            xp = _merge_mlp(xp, os_, ls_, wo, li, g_mlp, wu, wd, layer, tm=ROW_TILE, dils=dils)
            attn = jnp.transpose(sample_attn, (2, 0, 1, 3)).reshape(2 * N_GROUPS, 1, 1, nb, GROUP_W)
            xs = _merge(xs[None], attn, wo, li, tm=nb)[0]
            xs = _mlp(xs, g_mlp, wu, wd, layer, nb)

    kv_s, kv_p = ([jnp.transpose(c.reshape(c.shape[0], c.shape[1], 2, HEADS, HEAD_DIM, c.shape[-1]),
                                 (0, 1, 5, 2, 3, 4)) for c in group] for group in (new_caches, kv_tails))
    return (xp, xs.reshape(nb, 1, D_MODEL), jnp.stack(pool_p), jnp.stack(pool_s),
            kv_p[0], kv_s[0], kv_p[1], kv_s[1], kv_p[2], kv_s[2])
```
